```python
import jax, jax.numpy as jnp
from jax import lax
import numpy as np

D_MODEL = 2048
BATCH = 2
SEQ = 4096
DEPTH = 1

LRU_WIDTH = D_MODEL // 2
LRU_HEADS = 8
LRU_BLOCK = LRU_WIDTH // LRU_HEADS
LRU_CONV = 4
LRU_C = 8.0
N_HEADS = 8
N_KV_HEADS = 2
GQA = N_HEADS // N_KV_HEADS
HEAD_DIM = (D_MODEL - LRU_WIDTH) // N_HEADS
ATTN_WIDTH = N_HEADS * HEAD_DIM
KV_WIDTH = N_KV_HEADS * HEAD_DIM
MIX_WIDTH = LRU_WIDTH + ATTN_WIDTH
N_BRANCH = 3
CMP_BLOCK = 32
CMP_STRIDE = 16
SEL_BLOCK = 64
N_SEL = 16
N_LOCAL_SEL = 2
WINDOW = 512
Q_BLOCK = 128
IN_WIDTH = 2 * LRU_WIDTH + ATTN_WIDTH + 6 * KV_WIDTH + N_BRANCH * N_HEADS
D_FF = 3 * D_MODEL
FFN_CONV = 3
EPS = 1e-6
NEG_INF = -1e30
FORCE_SCORE = 1e4

kernel_name = 'hymba_rglru_nsa_convffn_block'


def rms_norm(x, g):
    xf = x.astype(jnp.float32)
    y = xf * lax.rsqrt(jnp.mean(xf * xf, axis=-1, keepdims=True) + EPS)
    return (y * g.astype(jnp.float32)).astype(x.dtype)


def causal_dwconv(x, w, b):
    k, c = w.shape
    y = lax.conv_general_dilated(x, w[:, None, :].astype(x.dtype), window_strides=(1,),
                                 padding=[(k - 1, 0)], dimension_numbers=('NWC', 'WIO', 'NWC'),
                                 feature_group_count=c)
    return y + b.astype(x.dtype)


def rg_lru(x, w_a, b_a, w_x, b_x, lam):
    bsz, s, c = x.shape
    xf = x.astype(jnp.float32)
    xh = xf.reshape(bsz, s, LRU_HEADS, LRU_BLOCK)
    r = jax.nn.sigmoid(jnp.einsum('bshi,hij->bshj', xh, w_a.astype(jnp.float32)).reshape(bsz, s, c) + b_a.astype(jnp.float32))
    i = jax.nn.sigmoid(jnp.einsum('bshi,hij->bshj', xh, w_x.astype(jnp.float32)).reshape(bsz, s, c) + b_x.astype(jnp.float32))
    log_a = -LRU_C * r * jax.nn.softplus(-lam.astype(jnp.float32))
    a = jnp.exp(log_a)
    u = jnp.sqrt(-jnp.expm1(2.0 * log_a)) * (i * xf)

    def combine(left, right):
        a_l, b_l = left
        a_r, b_r = right
        return a_l * a_r, a_r * b_l + b_r

    _, h = lax.associative_scan(combine, (a, u), axis=1)
    return h.astype(x.dtype)


def compress_kv(kv, pe, w1, b1, w2):
    bsz, s, hk, d = kv.shape
    n_c = (s - CMP_BLOCK) // CMP_STRIDE + 1
    idx = jnp.arange(n_c)[:, None] * CMP_STRIDE + jnp.arange(CMP_BLOCK)[None, :]
    blocks = kv[:, idx] + pe[:, None, :]
    flat = blocks.transpose(0, 1, 3, 2, 4).reshape(bsz, n_c, hk, CMP_BLOCK * d)
    return jax.nn.gelu(flat @ w1 + b1) @ w2


def nsa_attention(q, k_cmp, v_cmp, k_sel, v_sel, k_win, v_win, gate_logits):
    bsz, s = q.shape[0], q.shape[1]
    n_c = k_cmp.shape[1]
    n_s = s // SEL_BLOCK
    k_top = min(N_SEL, n_s)
    cmp_start = jnp.arange(n_c) * CMP_STRIDE
    cmp_end = cmp_start + CMP_BLOCK - 1
    sel_start = jnp.arange(n_s) * SEL_BLOCK
    overlap = jnp.clip(jnp.minimum(cmp_start[:, None] + CMP_BLOCK, sel_start[None, :] + SEL_BLOCK)
                       - jnp.maximum(cmp_start[:, None], sel_start[None, :]), 0).astype(jnp.float32) / CMP_BLOCK
    kb = k_sel.reshape(bsz, n_s, SEL_BLOCK, N_KV_HEADS, HEAD_DIM).transpose(0, 3, 1, 2, 4)
    vb = v_sel.reshape(bsz, n_s, SEL_BLOCK, N_KV_HEADS, HEAD_DIM).transpose(0, 3, 1, 2, 4)
    pad = ((0, 0), (WINDOW, 0), (0, 0), (0, 0))
    k_win_p = jnp.pad(k_win, pad)
    v_win_p = jnp.pad(v_win, pad)
    b_idx = jnp.arange(bsz)[:, None, None, None]
    h_idx = jnp.arange(N_KV_HEADS)[None, :, None, None]
    blk = jnp.arange(n_s)

    def block(c):
        qs = c * Q_BLOCK
        q_b = lax.dynamic_slice_in_dim(q, qs, Q_BLOCK, axis=1)
        g_b = lax.dynamic_slice_in_dim(gate_logits, qs, Q_BLOCK, axis=1)
        t = qs + jnp.arange(Q_BLOCK)
        valid_c = cmp_end[None, :] <= t[:, None]
        s_c = jnp.einsum('bqhgd,bnhd->bhgqn', q_b, k_cmp).astype(jnp.float32)
        p_c = jax.nn.softmax(jnp.where(valid_c, s_c, NEG_INF), axis=-1) * valid_c
        o_c = jnp.einsum('bhgqn,bnhd->bqhgd', p_c.astype(v_cmp.dtype), v_cmp)
        imp = jnp.einsum('bhgqn,ns->bhqs', p_c, overlap)
        cur = t // SEL_BLOCK
        causal_s = blk[None, :] <= cur[:, None]
        forced = (blk[None, :] == 0) | (causal_s & (blk[None, :] > cur[:, None] - N_LOCAL_SEL))
        score = jnp.where(forced, FORCE_SCORE, jnp.where(causal_s, imp, -1.0))
        _, sel = lax.top_k(score, k_top)
        k_g = kb[b_idx, h_idx, sel]
        v_g = vb[b_idx, h_idx, sel]
        kpos = sel[..., None] * SEL_BLOCK + jnp.arange(SEL_BLOCK)
        valid_s = (kpos <= t[None, None, :, None, None]).reshape(bsz, N_KV_HEADS, 1, Q_BLOCK, k_top * SEL_BLOCK)
        s_s = jnp.einsum('bqhgd,bhqjld->bhgqjl', q_b, k_g).reshape(bsz, N_KV_HEADS, GQA, Q_BLOCK, k_top * SEL_BLOCK)
        p_s = jax.nn.softmax(jnp.where(valid_s, s_s.astype(jnp.float32), NEG_INF), axis=-1)
        p_s = p_s.reshape(bsz, N_KV_HEADS, GQA, Q_BLOCK, k_top, SEL_BLOCK)
        o_s = jnp.einsum('bhgqjl,bhqjld->bqhgd', p_s.astype(v_g.dtype), v_g)
        k_w = lax.dynamic_slice_in_dim(k_win_p, qs, WINDOW + Q_BLOCK, axis=1)
        v_w = lax.dynamic_slice_in_dim(v_win_p, qs, WINDOW + Q_BLOCK, axis=1)
        kpos_w = qs - WINDOW + jnp.arange(WINDOW + Q_BLOCK)
        diff = t[:, None] - kpos_w[None, :]
        valid_w = (diff >= 0) & (diff < WINDOW) & (kpos_w[None, :] >= 0)
        s_w = jnp.einsum('bqhgd,bkhd->bhgqk', q_b, k_w).astype(jnp.float32)
        p_w = jax.nn.softmax(jnp.where(valid_w, s_w, NEG_INF), axis=-1)
        o_w = jnp.einsum('bhgqk,bkhd->bqhgd', p_w.astype(v_w.dtype), v_w)
        g = jax.nn.sigmoid(g_b.astype(jnp.float32))
        o = g[..., 0:1] * o_c + g[..., 1:2] * o_s + g[..., 2:3] * o_w
        return o.reshape(bsz, Q_BLOCK, ATTN_WIDTH).astype(q.dtype)

    out = lax.map(block, jnp.arange(s // Q_BLOCK))
    return out.transpose(1, 0, 2, 3).reshape(bsz, s, ATTN_WIDTH)


def setup_inputs(seed: int = 0) -> dict:
    key = jax.random.key(seed)
    ks = jax.random.split(key, 32)
    f32 = jnp.float32

    def nrm(k, shape, scale):
        return jax.random.normal(k, shape, f32) * scale

    def gain(k, n):
        return 1.0 + nrm(k, (DEPTH, n), 0.02)

    a0 = jax.random.uniform(ks[9], (DEPTH, LRU_WIDTH), f32, minval=0.9, maxval=0.999)
    return {
        'x': nrm(ks[0], (BATCH, SEQ, D_MODEL), 1.0),
        'g_mix': gain(ks[1], D_MODEL),
        'w_in': nrm(ks[2], (DEPTH, D_MODEL, IN_WIDTH), D_MODEL ** -0.5),
        'lru_conv_w': nrm(ks[3], (DEPTH, LRU_CONV, LRU_WIDTH), LRU_CONV ** -0.5),
        'lru_conv_b': nrm(ks[4], (DEPTH, LRU_WIDTH), 0.02),
        'lru_wa': nrm(ks[5], (DEPTH, LRU_HEADS, LRU_BLOCK, LRU_BLOCK), LRU_BLOCK ** -0.5),
        'lru_ba': nrm(ks[6], (DEPTH, LRU_WIDTH), 0.02),
        'lru_wx': nrm(ks[7], (DEPTH, LRU_HEADS, LRU_BLOCK, LRU_BLOCK), LRU_BLOCK ** -0.5),
        'lru_bx': nrm(ks[8], (DEPTH, LRU_WIDTH), 0.02),
        'lru_lambda': jnp.log(a0) - jnp.log1p(-a0),
        'cmp_pe_k': nrm(ks[10], (DEPTH, CMP_BLOCK, HEAD_DIM), 0.02),
        'cmp_w1_k': nrm(ks[11], (DEPTH, CMP_BLOCK * HEAD_DIM, HEAD_DIM), (CMP_BLOCK * HEAD_DIM) ** -0.5),
        'cmp_b1_k': nrm(ks[12], (DEPTH, HEAD_DIM), 0.02),
        'cmp_w2_k': nrm(ks[13], (DEPTH, HEAD_DIM, HEAD_DIM), HEAD_DIM ** -0.5),
        'cmp_pe_v': nrm(ks[14], (DEPTH, CMP_BLOCK, HEAD_DIM), 0.02),
        'cmp_w1_v': nrm(ks[15], (DEPTH, CMP_BLOCK * HEAD_DIM, HEAD_DIM), (CMP_BLOCK * HEAD_DIM) ** -0.5),
        'cmp_b1_v': nrm(ks[16], (DEPTH, HEAD_DIM), 0.02),
        'cmp_w2_v': nrm(ks[17], (DEPTH, HEAD_DIM, HEAD_DIM), HEAD_DIM ** -0.5),
        'g_lru_out': gain(ks[18], LRU_WIDTH),
        'g_attn_out': gain(ks[19], ATTN_WIDTH),
        'w_out': nrm(ks[20], (DEPTH, MIX_WIDTH, D_MODEL), MIX_WIDTH ** -0.5),
        'g_ffn': gain(ks[21], D_MODEL),
        'w_up': nrm(ks[22], (DEPTH, D_MODEL, 2 * D_FF), D_MODEL ** -0.5),
        'ffn_conv_w': nrm(ks[23], (DEPTH, FFN_CONV, D_FF), FFN_CONV ** -0.5),
        'ffn_conv_b': nrm(ks[24], (DEPTH, D_FF), 0.02),
        'w_down': nrm(ks[25], (DEPTH, D_FF, D_MODEL), D_FF ** -0.5),
        'g_final': 1.0 + nrm(ks[26], (D_MODEL,), 0.02),
    }


def reference(x, g_mix, w_in, lru_conv_w, lru_conv_b, lru_wa, lru_ba, lru_wx, lru_bx, lru_lambda,
              cmp_pe_k, cmp_w1_k, cmp_b1_k, cmp_w2_k, cmp_pe_v, cmp_w1_v, cmp_b1_v, cmp_w2_v,
              g_lru_out, g_attn_out, w_out, g_ffn, w_up, ffn_conv_w, ffn_conv_b, w_down, g_final):
    bsz, s, _ = x.shape
    cuts = np.cumsum([LRU_WIDTH, LRU_WIDTH, ATTN_WIDTH] + [KV_WIDTH] * 6).tolist()
    h = x
    for l in range(DEPTH):
        xn = rms_norm(h, g_mix[l])
        proj = xn @ w_in[l]
        lru_x, lru_gate, q, kc, vc, ksel, vsel, kw, vw, gate_logits = jnp.split(proj, cuts, axis=-1)
        u = causal_dwconv(lru_x, lru_conv_w[l], lru_conv_b[l])
        y_lru = rg_lru(u, lru_wa[l], lru_ba[l], lru_wx[l], lru_bx[l], lru_lambda[l]) * jax.nn.gelu(lru_gate)
        q = (q * (HEAD_DIM ** -0.5)).reshape(bsz, s, N_KV_HEADS, GQA, HEAD_DIM)
        kc = kc.reshape(bsz, s, N_KV_HEADS, HEAD_DIM)
        vc = vc.reshape(bsz, s, N_KV_HEADS, HEAD_DIM)
        k_cmp = compress_kv(kc, cmp_pe_k[l], cmp_w1_k[l], cmp_b1_k[l], cmp_w2_k[l])
        v_cmp = compress_kv(vc, cmp_pe_v[l], cmp_w1_v[l], cmp_b1_v[l], cmp_w2_v[l])
        y_attn = nsa_attention(q, k_cmp, v_cmp,
                               ksel.reshape(bsz, s, N_KV_HEADS, HEAD_DIM),
                               vsel.reshape(bsz, s, N_KV_HEADS, HEAD_DIM),
                               kw.reshape(bsz, s, N_KV_HEADS, HEAD_DIM),
                               vw.reshape(bsz, s, N_KV_HEADS, HEAD_DIM),
                               gate_logits.reshape(bsz, s, N_KV_HEADS, GQA, N_BRANCH))
        mixed = jnp.concatenate([rms_norm(y_lru, g_lru_out[l]), rms_norm(y_attn, g_attn_out[l])], axis=-1)
        h = h + mixed @ w_out[l]
        xn = rms_norm(h, g_ffn[l])
        u_ff, v_ff = jnp.split(xn @ w_up[l], 2, axis=-1)
        u_ff = causal_dwconv(u_ff, ffn_conv_w[l], ffn_conv_b[l])
        h = h + (jax.nn.gelu(u_ff) * v_ff) @ w_down[l]
    return rms_norm(h, g_final)
```

```python
import functools

import jax
import jax.numpy as jnp
from jax import lax
from jax.experimental import pallas as pl
from jax.experimental.pallas import tpu as pltpu

F32 = jnp.float32
BF16 = jnp.bfloat16

LANE = 128
SUBLANE = 8
VMEM_LIMIT = 56 * 1024 * 1024

LRU_HEADS = 8
LRU_CONV = 4
LRU_C = 8.0
N_HEADS = 8
N_KV_HEADS = 2
GQA = N_HEADS // N_KV_HEADS
HEAD_DIM = 128
N_BRANCH = 3
CMP_BLOCK = 32
CMP_STRIDE = 16
SEL_BLOCK = 64
N_SEL = 16
N_LOCAL_SEL = 2
WINDOW = 512
Q_BLOCK = 128
FFN_CONV = 3
EPS = 1e-6
NEG_INF = -1e30
FORCE_SCORE = 1e4


def _rms(x, g):
    return x * lax.rsqrt(jnp.mean(x * x, axis=-1, keepdims=True) + EPS) * g


def _cparams(sem):
    return pltpu.CompilerParams(dimension_semantics=sem, vmem_limit_bytes=VMEM_LIMIT)


def _inproj_kernel(x_ref, g_ref, w_ref, o_ref):
    xn = _rms(x_ref[...], g_ref[...]).astype(BF16)
    o_ref[...] = jnp.dot(xn, w_ref[...], preferred_element_type=F32)


def _in_proj(x2, g, w_pad, tm, tn):
    t, d = x2.shape
    n = w_pad.shape[1]
    return pl.pallas_call(
        _inproj_kernel,
        grid=(n // tn, t // tm),
        in_specs=[pl.BlockSpec((tm, d), lambda j, i: (i, 0)),
                  pl.BlockSpec((1, d), lambda j, i: (0, 0)),
                  pl.BlockSpec((d, tn), lambda j, i: (0, j))],
        out_specs=pl.BlockSpec((tm, tn), lambda j, i: (i, j)),
        out_shape=jax.ShapeDtypeStruct((t, n), F32),
        compiler_params=_cparams(("arbitrary", "arbitrary")),
        name="in_proj",
    )(x2, g, w_pad)


def _lru_kernel(x_ref, gate_ref, cw_ref, cb_ref, wa_ref, ba_ref, wx_ref, bx_ref, lam_ref, g_ref,
                o_ref, xpad_ref, carry_ref, a_ref, b_ref):
    ts, c = x_ref.shape
    hb = c // LRU_HEADS

    @pl.when(pl.program_id(1) == 0)
    def _():
        xpad_ref[...] = jnp.zeros_like(xpad_ref)
        carry_ref[...] = jnp.zeros_like(carry_ref)

    xpad_ref[0:SUBLANE, :] = xpad_ref[ts:ts + SUBLANE, :]
    xpad_ref[SUBLANE:, :] = x_ref[...]
    u = cb_ref[...] + cw_ref[LRU_CONV - 1:LRU_CONV, :] * x_ref[...]
    for k in range(LRU_CONV - 1):
        off = SUBLANE - (LRU_CONV - 1) + k
        u = u + cw_ref[k:k + 1, :] * xpad_ref[off:off + ts, :]

    ub = u.astype(BF16)
    r_parts, i_parts = [], []
    for h in range(LRU_HEADS):
        uh = ub[:, h * hb:(h + 1) * hb]
        r_parts.append(jnp.dot(uh, wa_ref[h], preferred_element_type=F32))
        i_parts.append(jnp.dot(uh, wx_ref[h], preferred_element_type=F32))
    r = jax.nn.sigmoid(jnp.concatenate(r_parts, axis=1) + ba_ref[...])
    ig = jax.nn.sigmoid(jnp.concatenate(i_parts, axis=1) + bx_ref[...])
    nl = -lam_ref[...]
    softplus = jnp.maximum(nl, 0.0) + jnp.log1p(jnp.exp(-jnp.abs(nl)))
    log_a = (-LRU_C) * r * softplus
    a = jnp.exp(log_a)
    a_ref[...] = a
    b_ref[...] = jnp.sqrt(-jnp.tanh(log_a) * (a * a + 1.0)) * (ig * u)

    row = lax.broadcasted_iota(jnp.int32, (SUBLANE, c), 0)

    def group(gi, carry):
        r0 = pl.multiple_of(gi * SUBLANE, SUBLANE)
        a8 = a_ref[pl.ds(r0, SUBLANE), :]
        b8 = b_ref[pl.ds(r0, SUBLANE), :]
        for d in (1, 2, 4):
            a_sh = pltpu.roll(a8, d, 0)
            b_sh = pltpu.roll(b8, d, 0)
            m = row >= d
            b8 = jnp.where(m, a8 * b_sh + b8, b8)
            a8 = jnp.where(m, a8 * a_sh, a8)
        h8 = a8 * carry + b8
        b_ref[pl.ds(r0, SUBLANE), :] = h8
        return jnp.broadcast_to(h8[SUBLANE - 1:SUBLANE, :], (SUBLANE, c))

    carry_ref[...] = lax.fori_loop(0, ts // SUBLANE, group, carry_ref[...])

    y = b_ref[...] * jax.nn.gelu(gate_ref[...])
    o_ref[...] = _rms(y, g_ref[...]).astype(o_ref.dtype)


def _lru(proj, bsz, s, cw, cb, wa, ba, wx, bx, lam, g, ts):
    c = cw.shape[1]
    ns = s // ts
    row = lambda b, i: b * ns + i
    vec = pl.BlockSpec((1, c), lambda b, i: (0, 0))
    mat = pl.BlockSpec(wa.shape, lambda b, i: (0, 0, 0))
    return pl.pallas_call(
        _lru_kernel,
        grid=(bsz, ns),
        in_specs=[pl.BlockSpec((ts, c), lambda b, i: (row(b, i), 0)),
                  pl.BlockSpec((ts, c), lambda b, i: (row(b, i), 1)),
                  pl.BlockSpec((LRU_CONV, c), lambda b, i: (0, 0)), vec,
                  mat, vec, mat, vec, vec, vec],
        out_specs=pl.BlockSpec((ts, c), lambda b, i: (row(b, i), 0)),
        out_shape=jax.ShapeDtypeStruct((bsz * s, c), BF16),
        scratch_shapes=[pltpu.VMEM((ts + SUBLANE, c), F32),
                        pltpu.VMEM((SUBLANE, c), F32),
                        pltpu.VMEM((ts, c), F32),
                        pltpu.VMEM((ts, c), F32)],
        compiler_params=_cparams(("arbitrary", "arbitrary")),
        name="rg_lru",
    )(proj, proj, cw, cb, wa, ba, wx, bx, lam, g)


def _compress_one(kv_ref, pe_ref, w1_ref, b1_ref, w2_ref):
    s, dh = kv_ref.shape
    n = s // CMP_STRIDE
    half = CMP_STRIDE * dh
    xa = jnp.concatenate([kv_ref[pl.ds(l, n, stride=CMP_STRIDE), :] for l in range(CMP_STRIDE)],
                         axis=1).astype(BF16)
    lo = jnp.dot(xa, w1_ref[0:half, :], preferred_element_type=F32)
    hi = jnp.dot(xa, w1_ref[half:2 * half, :], preferred_element_type=F32)
    pe = jnp.broadcast_to(pe_ref[...], (SUBLANE, 2 * half)).astype(BF16)
    pe_term = jnp.dot(pe, w1_ref[...], preferred_element_type=F32)[0:1, :]
    pre = lo + pltpu.roll(hi, n - 1, 0) + pe_term + b1_ref[...]
    return jnp.dot(jax.nn.gelu(pre).astype(BF16), w2_ref[...], preferred_element_type=F32)


def _compress_kernel(kc_ref, vc_ref, pek_ref, w1k_ref, b1k_ref, w2k_ref,
                     pev_ref, w1v_ref, b1v_ref, w2v_ref, ko_ref, vo_ref):
    ko_ref[0] = _compress_one(kc_ref, pek_ref, w1k_ref, b1k_ref, w2k_ref)
    vo_ref[0] = _compress_one(vc_ref, pev_ref, w1v_ref, b1v_ref, w2v_ref)


def _compress(proj, bsz, s, kc_blk, vc_blk, pk, pv):
    dh = HEAD_DIM
    n = s // CMP_STRIDE
    full = lambda a: pl.BlockSpec(a.shape, lambda b, h: (0,) * a.ndim)
    out = pl.BlockSpec((1, n, dh), lambda b, h: (b * N_KV_HEADS + h, 0, 0))
    shp = jax.ShapeDtypeStruct((bsz * N_KV_HEADS, n, dh), F32)
    return pl.pallas_call(
        _compress_kernel,
        grid=(bsz, N_KV_HEADS),
        in_specs=[pl.BlockSpec((s, dh), lambda b, h: (b, kc_blk + h)),
                  pl.BlockSpec((s, dh), lambda b, h: (b, vc_blk + h)),
                  *[full(a) for a in pk], *[full(a) for a in pv]],
        out_specs=[out, out],
        out_shape=[shp, shp],
        compiler_params=_cparams(("arbitrary", "arbitrary")),
        name="compress_kv",
    )(proj, proj, *pk, *pv)


KEY_CHUNK = 512


def _softmax_rows(s):
    e = jnp.exp(s - jnp.max(s, axis=1, keepdims=True))
    return e / jnp.sum(e, axis=1, keepdims=True)


def _attn_kernel(q_ref, ksel_ref, vsel_ref, kw_ref, vw_ref, gate_ref, kcmp_ref, vcmp_ref, ovt_ref,
                 o_ref, kaug_ref, vselb_ref, kwb_ref, vwb_ref):
    i = pl.program_id(2)
    s_len, dh = ksel_ref.shape
    tq = q_ref.shape[0]
    rows = GQA * tq
    n_cmp = kcmp_ref.shape[1]
    n_sel = s_len // SEL_BLOCK

    @pl.when(i == 0)
    def _():
        kaug_ref[:, 0:dh] = ksel_ref[...].astype(BF16)
        kblk = lax.broadcasted_iota(jnp.int32, (s_len, LANE), 0) // SEL_BLOCK
        lane = lax.broadcasted_iota(jnp.int32, (s_len, LANE), 1)
        kaug_ref[:, dh:dh + LANE] = jnp.where(kblk == lane, 1.0, 0.0).astype(BF16)
        vselb_ref[...] = vsel_ref[...].astype(BF16)
        kwb_ref[...] = kw_ref[...].astype(BF16)
        vwb_ref[...] = vw_ref[...].astype(BF16)

    qs = i * tq
    qf = q_ref[...] * (HEAD_DIM ** -0.5)
    qb = jnp.concatenate([qf[:, g * dh:(g + 1) * dh] for g in range(GQA)], axis=0).astype(BF16)
    t_row = qs + (lax.broadcasted_iota(jnp.int32, (rows, 1), 0) & (tq - 1))

    nt = (((1,), (1,)), ((), ()))
    s_c = lax.dot_general(qb, kcmp_ref[0].astype(BF16), nt, preferred_element_type=F32)
    cmp_end = lax.broadcasted_iota(jnp.int32, (rows, n_cmp), 1) * CMP_STRIDE + (CMP_BLOCK - 1)
    vis_c = cmp_end <= t_row
    s_c = jnp.where(vis_c, s_c, NEG_INF)
    e_c = jnp.where(vis_c, jnp.exp(s_c - jnp.max(s_c, axis=1, keepdims=True)), 0.0)
    den_c = jnp.sum(e_c, axis=1, keepdims=True)
    p_c = e_c / jnp.where(den_c > 0.0, den_c, 1.0)
    o_c = jnp.dot(p_c.astype(BF16), vcmp_ref[0].astype(BF16), preferred_element_type=F32)

    p_sum = p_c[0:tq]
    for g in range(1, GQA):
        p_sum = p_sum + p_c[g * tq:(g + 1) * tq]
    imp_t = lax.dot_general(ovt_ref[...], p_sum, nt, precision=lax.Precision.HIGHEST,
                            preferred_element_type=F32)
    blk = lax.broadcasted_iota(jnp.int32, (n_sel, tq), 0)
    cur = (qs + lax.broadcasted_iota(jnp.int32, (n_sel, tq), 1)) // SEL_BLOCK
    age = cur - blk
    score = jnp.where(blk == 0, FORCE_SCORE,
                      jnp.where(age < 0, -1.0, jnp.where(age < N_LOCAL_SEL, FORCE_SCORE, imp_t)))
    rank = jnp.zeros((n_sel, tq), jnp.int32)
    for j in range(n_sel):
        rj = score[j:j + 1, :]
        rank = rank + jnp.where(blk > j, jnp.where(rj >= score, 1, 0), jnp.where(rj > score, 1, 0))
    bias_t = jnp.where(rank < min(N_SEL, n_sel), 0.0, NEG_INF)
    bias_t = jnp.concatenate([bias_t, jnp.zeros((LANE - n_sel, tq), F32)], axis=0)
    bias = bias_t.T.astype(BF16)
    q_aug = jnp.concatenate([qb, jnp.concatenate([bias] * GQA, axis=0)], axis=1)

    def chunk(c, carry):
        m, l, acc = carry
        k0 = pl.multiple_of(c * KEY_CHUNK, KEY_CHUNK)
        s = lax.dot_general(q_aug, kaug_ref[pl.ds(k0, KEY_CHUNK), :], nt, preferred_element_type=F32)
        kpos = k0 + lax.broadcasted_iota(jnp.int32, (rows, KEY_CHUNK), 1)
        s = jnp.where(kpos <= t_row, s, NEG_INF)
        m_new = jnp.maximum(m, jnp.max(s, axis=1, keepdims=True))
        alpha = jnp.exp(m - m_new)
        p = jnp.exp(s - m_new)
        l = alpha * l + jnp.sum(p, axis=1, keepdims=True)
        acc = alpha * acc + jnp.dot(p.astype(BF16), vselb_ref[pl.ds(k0, KEY_CHUNK), :],
                                    preferred_element_type=F32)
        return m_new, l, acc

    n_chunks = (qs + tq + KEY_CHUNK - 1) // KEY_CHUNK
    init = (jnp.full((rows, 1), NEG_INF, F32), jnp.zeros((rows, 1), F32), jnp.zeros((rows, dh), F32))
    _, l_s, acc_s = lax.fori_loop(0, n_chunks, chunk, init)
    o_s = acc_s / l_s

    wlen = min(WINDOW + tq, s_len)
    w0 = pl.multiple_of(jnp.maximum(qs + tq - wlen, 0), tq)
    s_w = lax.dot_general(qb, kwb_ref[pl.ds(w0, wlen), :], nt, preferred_element_type=F32)
    diff = t_row - (w0 + lax.broadcasted_iota(jnp.int32, (rows, wlen), 1))
    p_w = _softmax_rows(jnp.where(diff >= 0, jnp.where(diff < WINDOW, s_w, NEG_INF), NEG_INF))
    o_w = jnp.dot(p_w.astype(BF16), vwb_ref[pl.ds(w0, wlen), :], preferred_element_type=F32)

    gate = jax.nn.sigmoid(gate_ref[...])
    for g in range(GQA):
        sl = slice(g * tq, (g + 1) * tq)
        o_ref[:, g * dh:(g + 1) * dh] = (gate[:, N_BRANCH * g:N_BRANCH * g + 1] * o_c[sl]
                                         + gate[:, N_BRANCH * g + 1:N_BRANCH * g + 2] * o_s[sl]
                                         + gate[:, N_BRANCH * g + 2:N_BRANCH * g + 3] * o_w[sl])


def _attention(proj, k_cmp, v_cmp, ov_t, bsz, s, blocks):
    dh = HEAD_DIM
    tq = Q_BLOCK
    nq = s // tq
    n_cmp = k_cmp.shape[1]
    kv = lambda blk: pl.BlockSpec((s, dh), lambda b, h, i: (b, blk + h))
    cmp_spec = pl.BlockSpec((1, n_cmp, dh), lambda b, h, i: (b * N_KV_HEADS + h, 0, 0))
    return pl.pallas_call(
        _attn_kernel,
        grid=(bsz, N_KV_HEADS, nq),
        in_specs=[pl.BlockSpec((tq, GQA * dh), lambda b, h, i: (b * nq + i, blocks["q"] + h)),
                  kv(blocks["ksel"]), kv(blocks["vsel"]), kv(blocks["kw"]), kv(blocks["vw"]),
                  pl.BlockSpec((tq, LANE), lambda b, h, i: (b * nq + i, blocks["gate"] + h)),
                  cmp_spec, cmp_spec,
                  pl.BlockSpec(ov_t.shape, lambda b, h, i: (0, 0))],
        out_specs=pl.BlockSpec((tq, GQA * dh), lambda b, h, i: (b * nq + i, h)),
        out_shape=jax.ShapeDtypeStruct((bsz * s, N_HEADS * dh), F32),
        scratch_shapes=[pltpu.VMEM((s, dh + LANE), BF16),
                        pltpu.VMEM((s, dh), BF16),
                        pltpu.VMEM((s, dh), BF16),
                        pltpu.VMEM((s, dh), BF16)],
        compiler_params=_cparams(("arbitrary", "arbitrary", "arbitrary")),
        name="nsa_attention",
    )(proj, proj, proj, proj, proj, proj, k_cmp, v_cmp, ov_t)


def _outproj_kernel(lru_ref, attn_ref, ga_ref, x_ref, w_ref, gf_ref, h_ref, xn_ref):
    c = lru_ref.shape[1]
    an = _rms(attn_ref[...], ga_ref[...]).astype(BF16)
    h = (x_ref[...]
         + jnp.dot(lru_ref[...], w_ref[0:c, :], preferred_element_type=F32)
         + jnp.dot(an, w_ref[c:, :], preferred_element_type=F32))
    h_ref[...] = h
    xn_ref[...] = _rms(h, gf_ref[...]).astype(BF16)


def _out_proj(lru_n, attn, g_attn, x2, w_out, g_ffn, tm):
    t, d = x2.shape
    c = lru_n.shape[1]
    return pl.pallas_call(
        _outproj_kernel,
        grid=(t // tm,),
        in_specs=[pl.BlockSpec((tm, c), lambda i: (i, 0)),
                  pl.BlockSpec((tm, attn.shape[1]), lambda i: (i, 0)),
                  pl.BlockSpec((1, attn.shape[1]), lambda i: (0, 0)),
                  pl.BlockSpec((tm, d), lambda i: (i, 0)),
                  pl.BlockSpec(w_out.shape, lambda i: (0, 0)),
                  pl.BlockSpec((1, d), lambda i: (0, 0))],
        out_specs=[pl.BlockSpec((tm, d), lambda i: (i, 0)),
                   pl.BlockSpec((tm, d), lambda i: (i, 0))],
        out_shape=[jax.ShapeDtypeStruct((t, d), F32), jax.ShapeDtypeStruct((t, d), BF16)],
        compiler_params=_cparams(("arbitrary",)),
        name="out_proj",
    )(lru_n, attn, g_attn, x2, w_out, g_ffn)


def _ffn_kernel(xn_ref, wu_ref, wv_ref, cw_ref, cb_ref, wd_ref, h_ref, g_ref, o_ref,
                acc_ref, halo_ref, upad_ref, *, tiles_per_seq, final_norm):
    i = pl.program_id(0)
    j = pl.program_id(1)
    tm = xn_ref.shape[0]

    @pl.when(j == 0)
    def _():
        acc_ref[...] = jnp.zeros_like(acc_ref)

    xn = xn_ref[...]
    u = jnp.dot(xn, wu_ref[...], preferred_element_type=F32)
    v = jnp.dot(xn, wv_ref[...], preferred_element_type=F32)
    first = (i % tiles_per_seq) == 0
    upad_ref[0:SUBLANE, :] = jnp.where(first, 0.0, halo_ref[j])
    upad_ref[SUBLANE:, :] = u
    halo_ref[j] = u[tm - SUBLANE:, :]
    uc = cb_ref[...] + cw_ref[FFN_CONV - 1:FFN_CONV, :] * u
    for k in range(FFN_CONV - 1):
        off = SUBLANE - (FFN_CONV - 1) + k
        uc = uc + cw_ref[k:k + 1, :] * upad_ref[off:off + tm, :]
    act = (jax.nn.gelu(uc) * v).astype(BF16)
    acc_ref[...] += jnp.dot(act, wd_ref[...], preferred_element_type=F32)

    @pl.when(j == pl.num_programs(1) - 1)
    def _():
        h = h_ref[...] + acc_ref[...]
        o_ref[...] = _rms(h, g_ref[...]) if final_norm else h


def _ffn(xn, h1, w_up, cw, cb, w_down, g_final, final_norm, s, tm, tf):
    t, d = xn.shape
    dff = w_down.shape[0]
    nj = dff // tf
    return pl.pallas_call(
        functools.partial(_ffn_kernel, tiles_per_seq=s // tm, final_norm=final_norm),
        grid=(t // tm, nj),
        in_specs=[pl.BlockSpec((tm, d), lambda i, j: (i, 0)),
                  pl.BlockSpec((d, tf), lambda i, j: (0, j)),
                  pl.BlockSpec((d, tf), lambda i, j: (0, nj + j)),
                  pl.BlockSpec((FFN_CONV, tf), lambda i, j: (0, j)),
                  pl.BlockSpec((1, tf), lambda i, j: (0, j)),
                  pl.BlockSpec((tf, d), lambda i, j: (j, 0)),
                  pl.BlockSpec((tm, d), lambda i, j: (i, 0)),
                  pl.BlockSpec((1, d), lambda i, j: (0, 0))],
        out_specs=pl.BlockSpec((tm, d), lambda i, j: (i, 0)),
        out_shape=jax.ShapeDtypeStruct((t, d), F32),
        scratch_shapes=[pltpu.VMEM((tm, d), F32),
                        pltpu.VMEM((nj, SUBLANE, tf), F32),
                        pltpu.VMEM((tm + SUBLANE, tf), F32)],
        compiler_params=_cparams(("arbitrary", "arbitrary")),
        name="conv_ffn",
    )(xn, w_up, w_up, cw, cb, w_down, h1, g_final)


def _overlap_t(n_cmp, n_sel):
    c0 = jnp.arange(n_cmp)[None, :] * CMP_STRIDE
    s0 = jnp.arange(n_sel)[:, None] * SEL_BLOCK
    ov = jnp.clip(jnp.minimum(c0 + CMP_BLOCK, s0 + SEL_BLOCK) - jnp.maximum(c0, s0), 0)
    return ov.astype(F32) / CMP_BLOCK


def kernel(x, g_mix, w_in, lru_conv_w, lru_conv_b, lru_wa, lru_ba, lru_wx, lru_bx, lru_lambda, cmp_pe_k, cmp_w1_k, cmp_b1_k, cmp_w2_k, cmp_pe_v, cmp_w1_v, cmp_b1_v, cmp_w2_v, g_lru_out, g_attn_out, w_out, g_ffn, w_up, ffn_conv_w, ffn_conv_b, w_down, g_final):
    bsz, s, d = x.shape
    depth = w_in.shape[0]
    c_lru = lru_conv_w.shape[2]
    attn_w = N_HEADS * HEAD_DIM
    kv_w = N_KV_HEADS * HEAD_DIM
    n_gate = N_BRANCH * GQA
    main_w = 2 * c_lru + attn_w + 6 * kv_w
    kv0 = (2 * c_lru + attn_w) // HEAD_DIM
    blocks = {"q": (2 * c_lru) // (GQA * HEAD_DIM), "kc": kv0, "vc": kv0 + 2, "ksel": kv0 + 4,
              "vsel": kv0 + 6, "kw": kv0 + 8, "vw": kv0 + 10, "gate": main_w // LANE}
    ov_t = _overlap_t(s // CMP_STRIDE, s // SEL_BLOCK)
    row = lambda a: a.reshape(1, -1)

    h = x.reshape(bsz * s, d)
    for l in range(depth):
        gl = w_in[l][:, main_w:]
        gpad = jnp.zeros((d, LANE - n_gate), F32)
        w_pad = jnp.concatenate([w_in[l][:, :main_w]]
                                + [p for hh in range(N_KV_HEADS)
                                   for p in (gl[:, hh * n_gate:(hh + 1) * n_gate], gpad)],
                                axis=1).astype(BF16)
        proj = _in_proj(h, row(g_mix[l]), w_pad, tm=512, tn=w_pad.shape[1] // 2)
        lru_n = _lru(proj, bsz, s, lru_conv_w[l], row(lru_conv_b[l]), lru_wa[l].astype(BF16), row(lru_ba[l]),
                     lru_wx[l].astype(BF16), row(lru_bx[l]), row(lru_lambda[l]), row(g_lru_out[l]), ts=256)
        pk = (cmp_pe_k[l].reshape(1, -1), cmp_w1_k[l].astype(BF16), row(cmp_b1_k[l]), cmp_w2_k[l].astype(BF16))
        pv = (cmp_pe_v[l].reshape(1, -1), cmp_w1_v[l].astype(BF16), row(cmp_b1_v[l]), cmp_w2_v[l].astype(BF16))
        k_cmp, v_cmp = _compress(proj, bsz, s, blocks["kc"], blocks["vc"], pk, pv)
        attn = _attention(proj, k_cmp, v_cmp, ov_t, bsz, s, blocks)
        h1, xn = _out_proj(lru_n, attn, row(g_attn_out[l]), h, w_out[l].astype(BF16), row(g_ffn[l]), tm=256)
        h = _ffn(xn, h1, w_up[l].astype(BF16), ffn_conv_w[l], row(ffn_conv_b[l]), w_down[l].astype(BF16),
                 row(g_final), l == depth - 1, s, tm=512, tf=512)
    return h.reshape(bsz, s, d)
```

```python
import functools

import jax
import jax.numpy as jnp
from jax import lax
from jax.experimental import pallas as pl
from jax.experimental.pallas import tpu as pltpu

F32 = jnp.float32
BF16 = jnp.bfloat16

LANE = 128
SUBLANE = 8
VMEM_LIMIT = 56 * 1024 * 1024

LRU_HEADS = 8
LRU_CONV = 4
LRU_C = 8.0
N_HEADS = 8
N_KV_HEADS = 2
GQA = N_HEADS // N_KV_HEADS
HEAD_DIM = 128
N_BRANCH = 3
CMP_BLOCK = 32
CMP_STRIDE = 16
SEL_BLOCK = 64
N_SEL = 16
N_LOCAL_SEL = 2
WINDOW = 512
Q_BLOCK = 128
FFN_CONV = 3
EPS = 1e-6
NEG_INF = -1e30
FORCE_SCORE = 1e4


def _rms(x, g):
    return x * lax.rsqrt(jnp.mean(x * x, axis=-1, keepdims=True) + EPS) * g


def _cparams(sem):
    return pltpu.CompilerParams(dimension_semantics=sem, vmem_limit_bytes=VMEM_LIMIT)


def _inproj_kernel(x_ref, g_ref, w_ref, wg_ref, o_ref, og_ref):
    xn = _rms(x_ref[...], g_ref[...]).astype(BF16)
    o_ref[...] = jnp.dot(xn, w_ref[...], preferred_element_type=F32)
    og_ref[...] = jnp.dot(xn, wg_ref[...], preferred_element_type=F32)


def _in_proj(x2, g, w_main, w_gate, tm):
    t, d = x2.shape
    n, ng = w_main.shape[1], w_gate.shape[1]
    resident = lambda a: pl.BlockSpec(a.shape, lambda i: (0, 0), pipeline_mode=pl.Buffered(1))
    return pl.pallas_call(
        _inproj_kernel,
        grid=(t // tm,),
        in_specs=[pl.BlockSpec((tm, d), lambda i: (i, 0)),
                  pl.BlockSpec((1, d), lambda i: (0, 0)),
                  resident(w_main), resident(w_gate)],
        out_specs=[pl.BlockSpec((tm, n), lambda i: (i, 0)),
                   pl.BlockSpec((tm, ng), lambda i: (i, 0))],
        out_shape=[jax.ShapeDtypeStruct((t, n), F32), jax.ShapeDtypeStruct((t, ng), F32)],
        compiler_params=_cparams(("arbitrary",)),
        name="in_proj",
    )(x2, g, w_main, w_gate)


def _lru_kernel(x_ref, gate_ref, cw_ref, cb_ref, wa_ref, ba_ref, wx_ref, bx_ref, lam_ref, g_ref,
                o_ref, xpad_ref, carry_ref, a_ref, b_ref):
    ts, c = x_ref.shape
    hb = c // LRU_HEADS

    @pl.when(pl.program_id(1) == 0)
    def _():
        xpad_ref[...] = jnp.zeros_like(xpad_ref)
        carry_ref[...] = jnp.zeros_like(carry_ref)

    xpad_ref[0:SUBLANE, :] = xpad_ref[ts:ts + SUBLANE, :]
    xpad_ref[SUBLANE:, :] = x_ref[...]
    u = cb_ref[...] + cw_ref[LRU_CONV - 1:LRU_CONV, :] * x_ref[...]
    for k in range(LRU_CONV - 1):
        off = SUBLANE - (LRU_CONV - 1) + k
        u = u + cw_ref[k:k + 1, :] * xpad_ref[off:off + ts, :]

    ub = u.astype(BF16)
    r_parts, i_parts = [], []
    for h in range(LRU_HEADS):
        uh = ub[:, h * hb:(h + 1) * hb]
        r_parts.append(jnp.dot(uh, wa_ref[h], preferred_element_type=F32))
        i_parts.append(jnp.dot(uh, wx_ref[h], preferred_element_type=F32))
    r = jax.nn.sigmoid(jnp.concatenate(r_parts, axis=1) + ba_ref[...])
    ig = jax.nn.sigmoid(jnp.concatenate(i_parts, axis=1) + bx_ref[...])
    nl = -lam_ref[...]
    softplus = jnp.maximum(nl, 0.0) + jnp.log1p(jnp.exp(-jnp.abs(nl)))
    log_a = (-LRU_C) * r * softplus
    a = jnp.exp(log_a)
    a_ref[...] = a
    b_ref[...] = jnp.sqrt(-jnp.tanh(log_a) * (a * a + 1.0)) * (ig * u)

    row = lax.broadcasted_iota(jnp.int32, (SUBLANE, c), 0)

    def group(gi, carry):
        r0 = pl.multiple_of(gi * SUBLANE, SUBLANE)
        a8 = a_ref[pl.ds(r0, SUBLANE), :]
        b8 = b_ref[pl.ds(r0, SUBLANE), :]
        for d in (1, 2, 4):
            a_sh = pltpu.roll(a8, d, 0)
            b_sh = pltpu.roll(b8, d, 0)
            m = row >= d
            b8 = jnp.where(m, a8 * b_sh + b8, b8)
            a8 = jnp.where(m, a8 * a_sh, a8)
        h8 = a8 * carry + b8
        b_ref[pl.ds(r0, SUBLANE), :] = h8
        return jnp.broadcast_to(h8[SUBLANE - 1:SUBLANE, :], (SUBLANE, c))

    carry_ref[...] = lax.fori_loop(0, ts // SUBLANE, group, carry_ref[...])

    y = b_ref[...] * jax.nn.gelu(gate_ref[...])
    o_ref[...] = _rms(y, g_ref[...]).astype(o_ref.dtype)


def _lru(proj, bsz, s, cw, cb, wa, ba, wx, bx, lam, g, ts):
    c = cw.shape[1]
    ns = s // ts
    row = lambda b, i: b * ns + i
    vec = pl.BlockSpec((1, c), lambda b, i: (0, 0))
    mat = pl.BlockSpec(wa.shape, lambda b, i: (0, 0, 0))
    return pl.pallas_call(
        _lru_kernel,
        grid=(bsz, ns),
        in_specs=[pl.BlockSpec((ts, c), lambda b, i: (row(b, i), 0)),
                  pl.BlockSpec((ts, c), lambda b, i: (row(b, i), 1)),
                  pl.BlockSpec((LRU_CONV, c), lambda b, i: (0, 0)), vec,
                  mat, vec, mat, vec, vec, vec],
        out_specs=pl.BlockSpec((ts, c), lambda b, i: (row(b, i), 0)),
        out_shape=jax.ShapeDtypeStruct((bsz * s, c), BF16),
        scratch_shapes=[pltpu.VMEM((ts + SUBLANE, c), F32),
                        pltpu.VMEM((SUBLANE, c), F32),
                        pltpu.VMEM((ts, c), F32),
                        pltpu.VMEM((ts, c), F32)],
        compiler_params=_cparams(("arbitrary", "arbitrary")),
        name="rg_lru",
    )(proj, proj, cw, cb, wa, ba, wx, bx, lam, g)


def _compress_one(kv_ref, pe_ref, w1_ref, b1_ref, w2_ref):
    s, dh = kv_ref.shape
    n = s // CMP_STRIDE
    half = CMP_STRIDE * dh
    xa = jnp.concatenate([kv_ref[pl.ds(l, n, stride=CMP_STRIDE), :] for l in range(CMP_STRIDE)],
                         axis=1).astype(BF16)
    lo = jnp.dot(xa, w1_ref[0:half, :], preferred_element_type=F32)
    hi = jnp.dot(xa, w1_ref[half:2 * half, :], preferred_element_type=F32)
    pe = jnp.broadcast_to(pe_ref[...], (SUBLANE, 2 * half)).astype(BF16)
    pe_term = jnp.dot(pe, w1_ref[...], preferred_element_type=F32)[0:1, :]
    pre = lo + pltpu.roll(hi, n - 1, 0) + pe_term + b1_ref[...]
    return jnp.dot(jax.nn.gelu(pre).astype(BF16), w2_ref[...], preferred_element_type=F32)


def _compress_kernel(kc_ref, vc_ref, pek_ref, w1k_ref, b1k_ref, w2k_ref,
                     pev_ref, w1v_ref, b1v_ref, w2v_ref, ko_ref, vo_ref):
    ko_ref[0] = _compress_one(kc_ref, pek_ref, w1k_ref, b1k_ref, w2k_ref)
    vo_ref[0] = _compress_one(vc_ref, pev_ref, w1v_ref, b1v_ref, w2v_ref)


def _compress(proj, bsz, s, kc_blk, vc_blk, pk, pv):
    dh = HEAD_DIM
    n = s // CMP_STRIDE
    full = lambda a: pl.BlockSpec(a.shape, lambda b, h: (0,) * a.ndim)
    out = pl.BlockSpec((1, n, dh), lambda b, h: (b * N_KV_HEADS + h, 0, 0))
    shp = jax.ShapeDtypeStruct((bsz * N_KV_HEADS, n, dh), F32)
    return pl.pallas_call(
        _compress_kernel,
        grid=(bsz, N_KV_HEADS),
        in_specs=[pl.BlockSpec((s, dh), lambda b, h: (b, kc_blk + h)),
                  pl.BlockSpec((s, dh), lambda b, h: (b, vc_blk + h)),
                  *[full(a) for a in pk], *[full(a) for a in pv]],
        out_specs=[out, out],
        out_shape=[shp, shp],
        compiler_params=_cparams(("arbitrary", "arbitrary")),
        name="compress_kv",
    )(proj, proj, *pk, *pv)


KEY_CHUNK = 512
ROW_GROUPS = 2


def _attn_kernel(q_ref, ksel_ref, vsel_ref, kw_ref, vw_ref, gate_ref, kcmp_ref, vcmp_ref, ovt_ref,
                 o_ref, kaug_ref, vsaug_ref, kwb_ref, vwaug_ref, dbias_ref, wbias_ref,
                 qaug_ref, s_a, s_b, p_a, p_b, acc_ref, m_ref, a_ref):
    i = pl.program_id(2)
    s_len, dh = ksel_ref.shape
    tq = q_ref.shape[0]
    n_cmp = kcmp_ref.shape[1]
    n_sel = s_len // SEL_BLOCK
    gh = GQA // ROW_GROUPS
    rg = gh * tq

    @pl.when(i == 0)
    def _():
        kaug_ref[:, 0:dh] = ksel_ref[...].astype(BF16)
        kblk = lax.broadcasted_iota(jnp.int32, (s_len, LANE), 0) // SEL_BLOCK
        lane = lax.broadcasted_iota(jnp.int32, (s_len, LANE), 1)
        kaug_ref[:, dh:dh + LANE] = jnp.where(kblk == lane, 1.0, 0.0).astype(BF16)
        ones = jnp.ones((s_len, LANE), BF16)
        vsaug_ref[:, 0:dh] = vsel_ref[...].astype(BF16)
        vsaug_ref[:, dh:dh + LANE] = ones
        kwb_ref[...] = kw_ref[...].astype(BF16)
        vwaug_ref[:, 0:dh] = vw_ref[...].astype(BF16)
        vwaug_ref[:, dh:dh + LANE] = ones
        for v in range(dbias_ref.shape[0]):
            col = lax.broadcasted_iota(jnp.int32, (rg, KEY_CHUNK), 1)
            tok = lax.broadcasted_iota(jnp.int32, (rg, KEY_CHUNK), 0) & (tq - 1)
            dbias_ref[v] = jnp.where(col <= v * tq + tok, 0.0, NEG_INF)
        for v in range(wbias_ref.shape[0]):
            col = lax.broadcasted_iota(jnp.int32, wbias_ref.shape[1:], 1)
            tok = lax.broadcasted_iota(jnp.int32, wbias_ref.shape[1:], 0) & (tq - 1)
            d = v * tq + tok - col
            wbias_ref[v] = jnp.where(d >= 0, jnp.where(d < WINDOW, 0.0, NEG_INF), NEG_INF)

    nt = (((1,), (1,)), ((), ()))
    qs = i * tq
    qf = q_ref[...] * (HEAD_DIM ** -0.5)
    qb = [jnp.concatenate([qf[:, g * dh:(g + 1) * dh] for g in range(sp * gh, (sp + 1) * gh)],
                          axis=0).astype(BF16) for sp in range(ROW_GROUPS)]
    t_row = qs + (lax.broadcasted_iota(jnp.int32, (rg, 1), 0) & (tq - 1))

    wlen = wbias_ref.shape[2]
    w0 = pl.multiple_of(jnp.maximum(qs + tq - wlen, 0), tq)
    band_bias = wbias_ref[jnp.minimum(i, wbias_ref.shape[0] - 1)]
    o_w = []
    for sp in range(ROW_GROUPS):
        s_w = lax.dot_general(qb[sp], kwb_ref[pl.ds(w0, wlen), :], nt, preferred_element_type=F32) + band_bias
        e_w = jnp.exp(s_w - jnp.max(s_w, axis=1, keepdims=True)).astype(BF16)
        pv = jnp.dot(e_w, vwaug_ref[pl.ds(w0, wlen), :], preferred_element_type=F32)
        o_w.append(pv[:, :dh] / pv[:, dh:])

    kcb = kcmp_ref[0].astype(BF16)
    vcb = vcmp_ref[0].astype(BF16)
    vis_c = (lax.broadcasted_iota(jnp.int32, (rg, n_cmp), 1) * CMP_STRIDE + (CMP_BLOCK - 1)) <= t_row
    o_c, p_sum = [], None
    for sp in range(ROW_GROUPS):
        s_c = jnp.where(vis_c, lax.dot_general(qb[sp], kcb, nt, preferred_element_type=F32), NEG_INF)
        e_c = jnp.where(vis_c, jnp.exp(s_c - jnp.max(s_c, axis=1, keepdims=True)), 0.0)
        den_c = jnp.sum(e_c, axis=1, keepdims=True)
        p_c = e_c / jnp.where(den_c > 0.0, den_c, 1.0)
        o_c.append(jnp.dot(p_c.astype(BF16), vcb, preferred_element_type=F32))
        for g in range(gh):
            pg = p_c[g * tq:(g + 1) * tq]
            p_sum = pg if p_sum is None else p_sum + pg

    gate = jax.nn.sigmoid(gate_ref[...])
    for g in range(GQA):
        sp, gg = divmod(g, gh)
        sl = slice(gg * tq, (gg + 1) * tq)
        o_ref[:, g * dh:(g + 1) * dh] = (gate[:, N_BRANCH * g:N_BRANCH * g + 1] * o_c[sp][sl]
                                         + gate[:, N_BRANCH * g + 2:N_BRANCH * g + 3] * o_w[sp][sl])

    imp_t = lax.dot_general(ovt_ref[...], p_sum, nt, precision=lax.Precision.HIGHEST,
                            preferred_element_type=F32)
    blk = lax.broadcasted_iota(jnp.int32, (n_sel, tq), 0)
    cur = (qs + lax.broadcasted_iota(jnp.int32, (n_sel, tq), 1)) // SEL_BLOCK
    age = cur - blk
    score = jnp.where(blk == 0, FORCE_SCORE,
                      jnp.where(age < 0, -1.0, jnp.where(age < N_LOCAL_SEL, FORCE_SCORE, imp_t)))
    parts = [jnp.zeros((n_sel, tq), jnp.int32) for _ in range(4)]
    for j in range(n_sel):
        rj = score[j:j + 1, :]
        parts[j % 4] = parts[j % 4] + jnp.where(blk > j, jnp.where(rj >= score, 1, 0),
                                                jnp.where(rj > score, 1, 0))
    rank = (parts[0] + parts[1]) + (parts[2] + parts[3])
    bias_t = jnp.where(rank < min(N_SEL, n_sel), 0.0, NEG_INF)
    bias_t = jnp.concatenate([bias_t, jnp.zeros((LANE - n_sel, tq), F32)], axis=0)
    bias = jnp.concatenate([bias_t.T.astype(BF16)] * gh, axis=0)
    for sp in range(ROW_GROUPS):
        qaug_ref[sp] = jnp.concatenate([qb[sp], bias], axis=1)

    def scores(sp, k0):
        return lax.dot_general(qaug_ref[sp], kaug_ref[pl.ds(k0, KEY_CHUNK), :], nt,
                               preferred_element_type=F32)

    def accumulate(sp, p, alpha, k0):
        pv = jnp.dot(p, vsaug_ref[pl.ds(k0, KEY_CHUNK), :], preferred_element_type=F32)
        acc_ref[sp] = alpha * acc_ref[sp] + pv

    def soften(sp, s):
        m_old = m_ref[sp]
        m_new = jnp.maximum(m_old, jnp.max(s, axis=1, keepdims=True))
        m_ref[sp] = m_new
        return jnp.exp(m_old - m_new), jnp.exp(s - m_new).astype(BF16)

    def step(c, s_in, p_prev, s_out, p_out):
        k_next = pl.multiple_of((c + 1) * KEY_CHUNK, KEY_CHUNK)
        k_prev = pl.multiple_of(jnp.maximum(c - 1, 0) * KEY_CHUNK, KEY_CHUNK)
        for sp in range(ROW_GROUPS):
            accumulate(sp, p_prev[sp], a_ref[sp], k_prev)
        for sp in range(ROW_GROUPS):
            a_ref[sp], p_out[sp] = soften(sp, s_in[sp])
        for sp in range(ROW_GROUPS):
            s_out[sp] = scores(sp, k_next)

    def finish(n, s_in, p_prev):
        k_last = pl.multiple_of(jnp.maximum(n - 1, 0) * KEY_CHUNK, KEY_CHUNK)
        k_diag = pl.multiple_of(n * KEY_CHUNK, KEY_CHUNK)
        causal_bias = dbias_ref[i % (KEY_CHUNK // tq)]
        for sp in range(ROW_GROUPS):
            accumulate(sp, p_prev[sp], a_ref[sp], k_last)
            alpha, p = soften(sp, s_in[sp] + causal_bias)
            accumulate(sp, p, alpha, k_diag)

    n_full = qs // KEY_CHUNK
    for sp in range(ROW_GROUPS):
        s_a[sp] = scores(sp, 0)
        p_b[sp] = jnp.zeros((rg, KEY_CHUNK), BF16)
        a_ref[sp] = jnp.ones((rg, 1), F32)
        m_ref[sp] = jnp.full((rg, 1), NEG_INF, F32)
        acc_ref[sp] = jnp.zeros((rg, dh + LANE), F32)

    def pair(k, carry):
        step(2 * k, s_a, p_b, s_b, p_a)
        step(2 * k + 1, s_b, p_a, s_a, p_b)
        return carry

    lax.fori_loop(0, n_full // 2, pair, 0)

    @pl.when(n_full % 2 == 0)
    def _():
        finish(n_full, s_a, p_b)

    @pl.when(n_full % 2 == 1)
    def _():
        step(n_full - 1, s_a, p_b, s_b, p_a)
        finish(n_full, s_b, p_a)

    gate = jax.nn.sigmoid(gate_ref[...])
    for g in range(GQA):
        sp, gg = divmod(g, gh)
        acc = acc_ref[sp, gg * tq:(gg + 1) * tq, :]
        o_ref[:, g * dh:(g + 1) * dh] += gate[:, N_BRANCH * g + 1:N_BRANCH * g + 2] * (acc[:, :dh] / acc[:, dh:])


def _attention(proj, gates, k_cmp, v_cmp, ov_t, bsz, s, blocks):
    dh = HEAD_DIM
    tq = Q_BLOCK
    nq = s // tq
    rg = (GQA // ROW_GROUPS) * tq
    n_cmp = k_cmp.shape[1]
    kv = lambda blk: pl.BlockSpec((s, dh), lambda b, h, i: (b, blk + h))
    cmp_spec = pl.BlockSpec((1, n_cmp, dh), lambda b, h, i: (b * N_KV_HEADS + h, 0, 0))
    return pl.pallas_call(
        _attn_kernel,
        grid=(bsz, N_KV_HEADS, nq),
        in_specs=[pl.BlockSpec((tq, GQA * dh), lambda b, h, i: (b * nq + i, blocks["q"] + h)),
                  kv(blocks["ksel"]), kv(blocks["vsel"]), kv(blocks["kw"]), kv(blocks["vw"]),
                  pl.BlockSpec((tq, LANE), lambda b, h, i: (b * nq + i, h)),
                  cmp_spec, cmp_spec,
                  pl.BlockSpec(ov_t.shape, lambda b, h, i: (0, 0))],
        out_specs=pl.BlockSpec((tq, GQA * dh), lambda b, h, i: (b * nq + i, h)),
        out_shape=jax.ShapeDtypeStruct((bsz * s, N_HEADS * dh), F32),
        scratch_shapes=[pltpu.VMEM((s, dh + LANE), BF16),
                        pltpu.VMEM((s, dh + LANE), BF16),
                        pltpu.VMEM((s, dh), BF16),
                        pltpu.VMEM((s, dh + LANE), BF16),
                        pltpu.VMEM((KEY_CHUNK // tq, rg, KEY_CHUNK), F32),
                        pltpu.VMEM((WINDOW // tq + 1, rg, min(WINDOW + tq, s)), F32),
                        pltpu.VMEM((ROW_GROUPS, rg, dh + LANE), BF16),
                        pltpu.VMEM((ROW_GROUPS, rg, KEY_CHUNK), F32),
                        pltpu.VMEM((ROW_GROUPS, rg, KEY_CHUNK), F32),
                        pltpu.VMEM((ROW_GROUPS, rg, KEY_CHUNK), BF16),
                        pltpu.VMEM((ROW_GROUPS, rg, KEY_CHUNK), BF16),
                        pltpu.VMEM((ROW_GROUPS, rg, dh + LANE), F32),
                        pltpu.VMEM((ROW_GROUPS, rg, 1), F32),
                        pltpu.VMEM((ROW_GROUPS, rg, 1), F32)],
        compiler_params=_cparams(("arbitrary", "arbitrary", "arbitrary")),
        name="nsa_attention",
    )(proj, proj, proj, proj, proj, gates, k_cmp, v_cmp, ov_t)


def _outproj_kernel(lru_ref, attn_ref, ga_ref, x_ref, w_ref, gf_ref, h_ref, xn_ref):
    c = lru_ref.shape[1]
    an = _rms(attn_ref[...], ga_ref[...]).astype(BF16)
    h = (x_ref[...]
         + jnp.dot(lru_ref[...], w_ref[0:c, :], preferred_element_type=F32)
         + jnp.dot(an, w_ref[c:, :], preferred_element_type=F32))
    h_ref[...] = h
    xn_ref[...] = _rms(h, gf_ref[...]).astype(BF16)


def _out_proj(lru_n, attn, g_attn, x2, w_out, g_ffn, tm):
    t, d = x2.shape
    c = lru_n.shape[1]
    return pl.pallas_call(
        _outproj_kernel,
        grid=(t // tm,),
        in_specs=[pl.BlockSpec((tm, c), lambda i: (i, 0)),
                  pl.BlockSpec((tm, attn.shape[1]), lambda i: (i, 0)),
                  pl.BlockSpec((1, attn.shape[1]), lambda i: (0, 0)),
                  pl.BlockSpec((tm, d), lambda i: (i, 0)),
                  pl.BlockSpec(w_out.shape, lambda i: (0, 0), pipeline_mode=pl.Buffered(1)),
                  pl.BlockSpec((1, d), lambda i: (0, 0))],
        out_specs=[pl.BlockSpec((tm, d), lambda i: (i, 0)),
                   pl.BlockSpec((tm, d), lambda i: (i, 0))],
        out_shape=[jax.ShapeDtypeStruct((t, d), F32), jax.ShapeDtypeStruct((t, d), BF16)],
        compiler_params=_cparams(("arbitrary",)),
        name="out_proj",
    )(lru_n, attn, g_attn, x2, w_out, g_ffn)


def _ffn_kernel(xn_ref, wu_ref, wv_ref, cw_ref, cb_ref, wd_ref, h_ref, g_ref, o_ref,
                acc_ref, halo_ref, upad_ref, *, tiles_per_seq, final_norm):
    i = pl.program_id(0)
    j = pl.program_id(1)
    tm = xn_ref.shape[0]

    @pl.when(j == 0)
    def _():
        acc_ref[...] = jnp.zeros_like(acc_ref)

    xn = xn_ref[...]
    u = jnp.dot(xn, wu_ref[...], preferred_element_type=F32)
    v = jnp.dot(xn, wv_ref[...], preferred_element_type=F32)
    first = (i % tiles_per_seq) == 0
    upad_ref[0:SUBLANE, :] = jnp.where(first, 0.0, halo_ref[j])
    upad_ref[SUBLANE:, :] = u
    halo_ref[j] = u[tm - SUBLANE:, :]
    uc = cb_ref[...] + cw_ref[FFN_CONV - 1:FFN_CONV, :] * u
    for k in range(FFN_CONV - 1):
        off = SUBLANE - (FFN_CONV - 1) + k
        uc = uc + cw_ref[k:k + 1, :] * upad_ref[off:off + tm, :]
    act = (jax.nn.gelu(uc) * v).astype(BF16)
    acc_ref[...] += jnp.dot(act, wd_ref[...], preferred_element_type=F32)

    @pl.when(j == pl.num_programs(1) - 1)
    def _():
        h = h_ref[...] + acc_ref[...]
        o_ref[...] = _rms(h, g_ref[...]) if final_norm else h


def _ffn(xn, h1, w_up, cw, cb, w_down, g_final, final_norm, s, tm, tf):
    t, d = xn.shape
    dff = w_down.shape[0]
    nj = dff // tf
    return pl.pallas_call(
        functools.partial(_ffn_kernel, tiles_per_seq=s // tm, final_norm=final_norm),
        grid=(t // tm, nj),
        in_specs=[pl.BlockSpec((tm, d), lambda i, j: (i, 0)),
                  pl.BlockSpec((d, tf), lambda i, j: (0, j)),
                  pl.BlockSpec((d, tf), lambda i, j: (0, nj + j)),
                  pl.BlockSpec((FFN_CONV, tf), lambda i, j: (0, j)),
                  pl.BlockSpec((1, tf), lambda i, j: (0, j)),
                  pl.BlockSpec((tf, d), lambda i, j: (j, 0)),
                  pl.BlockSpec((tm, d), lambda i, j: (i, 0)),
                  pl.BlockSpec((1, d), lambda i, j: (0, 0))],
        out_specs=pl.BlockSpec((tm, d), lambda i, j: (i, 0)),
        out_shape=jax.ShapeDtypeStruct((t, d), F32),
        scratch_shapes=[pltpu.VMEM((tm, d), F32),
                        pltpu.VMEM((nj, SUBLANE, tf), F32),
                        pltpu.VMEM((tm + SUBLANE, tf), F32)],
        compiler_params=_cparams(("arbitrary", "arbitrary")),
        name="conv_ffn",
    )(xn, w_up, w_up, cw, cb, w_down, h1, g_final)


def _overlap_t(n_cmp, n_sel):
    c0 = jnp.arange(n_cmp)[None, :] * CMP_STRIDE
    s0 = jnp.arange(n_sel)[:, None] * SEL_BLOCK
    ov = jnp.clip(jnp.minimum(c0 + CMP_BLOCK, s0 + SEL_BLOCK) - jnp.maximum(c0, s0), 0)
    return ov.astype(F32) / CMP_BLOCK


def kernel(x, g_mix, w_in, lru_conv_w, lru_conv_b, lru_wa, lru_ba, lru_wx, lru_bx, lru_lambda, cmp_pe_k, cmp_w1_k, cmp_b1_k, cmp_w2_k, cmp_pe_v, cmp_w1_v, cmp_b1_v, cmp_w2_v, g_lru_out, g_attn_out, w_out, g_ffn, w_up, ffn_conv_w, ffn_conv_b, w_down, g_final):
    bsz, s, d = x.shape
    depth = w_in.shape[0]
    c_lru = lru_conv_w.shape[2]
    attn_w = N_HEADS * HEAD_DIM
    kv_w = N_KV_HEADS * HEAD_DIM
    n_gate = N_BRANCH * GQA
    main_w = 2 * c_lru + attn_w + 6 * kv_w
    kv0 = (2 * c_lru + attn_w) // HEAD_DIM
    blocks = {"q": (2 * c_lru) // (GQA * HEAD_DIM), "kc": kv0, "vc": kv0 + 2, "ksel": kv0 + 4,
              "vsel": kv0 + 6, "kw": kv0 + 8, "vw": kv0 + 10}
    ov_t = _overlap_t(s // CMP_STRIDE, s // SEL_BLOCK)
    row = lambda a: a.reshape(1, -1)

    h = x.reshape(bsz * s, d)
    for l in range(depth):
        w_main = w_in[l][:, :main_w].astype(BF16)
        gpad = jnp.zeros((d, LANE - n_gate), F32)
        w_gate = jnp.concatenate([p for hh in range(N_KV_HEADS)
                                  for p in (w_in[l][:, main_w + hh * n_gate:main_w + (hh + 1) * n_gate], gpad)],
                                 axis=1).astype(BF16)
        proj, gates = _in_proj(h, row(g_mix[l]), w_main, w_gate, tm=512)
        lru_n = _lru(proj, bsz, s, lru_conv_w[l], row(lru_conv_b[l]), lru_wa[l].astype(BF16), row(lru_ba[l]),
                     lru_wx[l].astype(BF16), row(lru_bx[l]), row(lru_lambda[l]), row(g_lru_out[l]), ts=256)
        pk = (cmp_pe_k[l].reshape(1, -1), cmp_w1_k[l].astype(BF16), row(cmp_b1_k[l]), cmp_w2_k[l].astype(BF16))
        pv = (cmp_pe_v[l].reshape(1, -1), cmp_w1_v[l].astype(BF16), row(cmp_b1_v[l]), cmp_w2_v[l].astype(BF16))
        k_cmp, v_cmp = _compress(proj, bsz, s, blocks["kc"], blocks["vc"], pk, pv)
        attn = _attention(proj, gates, k_cmp, v_cmp, ov_t, bsz, s, blocks)
        h1, xn = _out_proj(lru_n, attn, row(g_attn_out[l]), h, w_out[l].astype(BF16), row(g_ffn[l]), tm=512)
        h = _ffn(xn, h1, w_up[l].astype(BF16), ffn_conv_w[l], row(ffn_conv_b[l]), w_down[l].astype(BF16),
                 row(g_final), l == depth - 1, s, tm=512, tf=512)
    return h.reshape(bsz, s, d)
```

```python
import functools

import jax
import jax.numpy as jnp
from jax import lax
from jax.experimental import pallas as pl
from jax.experimental.pallas import tpu as pltpu

F32 = jnp.float32
BF16 = jnp.bfloat16

LANE = 128
SUBLANE = 8
VMEM_LIMIT = 56 * 1024 * 1024

LRU_HEADS = 8
LRU_CONV = 4
LRU_C = 8.0
N_HEADS = 8
N_KV_HEADS = 2
GQA = N_HEADS // N_KV_HEADS
HEAD_DIM = 128
N_BRANCH = 3
CMP_BLOCK = 32
CMP_STRIDE = 16
SEL_BLOCK = 64
N_SEL = 16
N_LOCAL_SEL = 2
WINDOW = 512
Q_BLOCK = 256
FFN_CONV = 3
EPS = 1e-6
NEG_INF = -1e30
FORCE_SCORE = 1e4


def _rms(x, g):
    return x * lax.rsqrt(jnp.mean(x * x, axis=-1, keepdims=True) + EPS) * g


def _cparams(sem):
    return pltpu.CompilerParams(dimension_semantics=sem, vmem_limit_bytes=VMEM_LIMIT)


def _inproj_kernel(x_ref, g_ref, w_ref, wg_ref, o_ref, og_ref):
    xn = _rms(x_ref[...], g_ref[...]).astype(BF16)
    o_ref[...] = jnp.dot(xn, w_ref[...], preferred_element_type=F32)
    og_ref[...] = jnp.dot(xn, wg_ref[...], preferred_element_type=F32)


def _in_proj(x2, g, w_all, n, w_gate, tm):
    t, d = x2.shape
    ng = w_gate.shape[1]
    resident = lambda shape: pl.BlockSpec(shape, lambda i: (0, 0), pipeline_mode=pl.Buffered(1))
    return pl.pallas_call(
        _inproj_kernel,
        grid=(t // tm,),
        in_specs=[pl.BlockSpec((tm, d), lambda i: (i, 0)),
                  pl.BlockSpec((1, d), lambda i: (0, 0)),
                  resident((d, n)), resident(w_gate.shape)],
        out_specs=[pl.BlockSpec((tm, n), lambda i: (i, 0)),
                   pl.BlockSpec((tm, ng), lambda i: (i, 0))],
        out_shape=[jax.ShapeDtypeStruct((t, n), F32), jax.ShapeDtypeStruct((t, ng), F32)],
        compiler_params=_cparams(("arbitrary",)),
        name="in_proj",
    )(x2, g, w_all, w_gate)


def _lru_kernel(x_ref, gate_ref, cw_ref, cb_ref, wa_ref, ba_ref, wx_ref, bx_ref, lam_ref, g_ref,
                o_ref, xpad_ref, carry_ref, a_ref, b_ref):
    ts, c = x_ref.shape
    hb = c // LRU_HEADS

    @pl.when(pl.program_id(1) == 0)
    def _():
        xpad_ref[...] = jnp.zeros_like(xpad_ref)
        carry_ref[...] = jnp.zeros_like(carry_ref)

    xpad_ref[0:SUBLANE, :] = xpad_ref[ts:ts + SUBLANE, :]
    xpad_ref[SUBLANE:, :] = x_ref[...]
    u = cb_ref[...] + cw_ref[LRU_CONV - 1:LRU_CONV, :] * x_ref[...]
    for k in range(LRU_CONV - 1):
        off = SUBLANE - (LRU_CONV - 1) + k
        u = u + cw_ref[k:k + 1, :] * xpad_ref[off:off + ts, :]

    ub = u.astype(BF16)
    r_parts, i_parts = [], []
    for h in range(LRU_HEADS):
        uh = ub[:, h * hb:(h + 1) * hb]
        r_parts.append(jnp.dot(uh, wa_ref[h], preferred_element_type=F32))
        i_parts.append(jnp.dot(uh, wx_ref[h], preferred_element_type=F32))
    r = jax.nn.sigmoid(jnp.concatenate(r_parts, axis=1) + ba_ref[...])
    ig = jax.nn.sigmoid(jnp.concatenate(i_parts, axis=1) + bx_ref[...])
    nl = -lam_ref[...]
    softplus = jnp.maximum(nl, 0.0) + jnp.log1p(jnp.exp(-jnp.abs(nl)))
    log_a = (-LRU_C) * r * softplus
    a = jnp.exp(log_a)
    a_ref[...] = a
    b_ref[...] = jnp.sqrt(-jnp.tanh(log_a) * (a * a + 1.0)) * (ig * u)

    row = lax.broadcasted_iota(jnp.int32, (SUBLANE, c), 0)

    def group(gi, carry):
        r0 = pl.multiple_of(gi * SUBLANE, SUBLANE)
        a8 = a_ref[pl.ds(r0, SUBLANE), :]
        b8 = b_ref[pl.ds(r0, SUBLANE), :]
        for d in (1, 2, 4):
            a_sh = pltpu.roll(a8, d, 0)
            b_sh = pltpu.roll(b8, d, 0)
            m = row >= d
            b8 = jnp.where(m, a8 * b_sh + b8, b8)
            a8 = jnp.where(m, a8 * a_sh, a8)
        h8 = a8 * carry + b8
        b_ref[pl.ds(r0, SUBLANE), :] = h8
        return jnp.broadcast_to(h8[SUBLANE - 1:SUBLANE, :], (SUBLANE, c))

    carry_ref[...] = lax.fori_loop(0, ts // SUBLANE, group, carry_ref[...])

    y = b_ref[...] * jax.nn.gelu(gate_ref[...])
    o_ref[...] = _rms(y, g_ref[...]).astype(o_ref.dtype)


def _lru(proj, bsz, s, cw, cb, wa, ba, wx, bx, lam, g, ts):
    c = cw.shape[1]
    ns = s // ts
    row = lambda b, i: b * ns + i
    vec = pl.BlockSpec((1, c), lambda b, i: (0, 0))
    mat = pl.BlockSpec(wa.shape, lambda b, i: (0, 0, 0))
    return pl.pallas_call(
        _lru_kernel,
        grid=(bsz, ns),
        in_specs=[pl.BlockSpec((ts, c), lambda b, i: (row(b, i), 0)),
                  pl.BlockSpec((ts, c), lambda b, i: (row(b, i), 1)),
                  pl.BlockSpec((LRU_CONV, c), lambda b, i: (0, 0)), vec,
                  mat, vec, mat, vec, vec, vec],
        out_specs=pl.BlockSpec((ts, c), lambda b, i: (row(b, i), 0)),
        out_shape=jax.ShapeDtypeStruct((bsz * s, c), BF16),
        scratch_shapes=[pltpu.VMEM((ts + SUBLANE, c), F32),
                        pltpu.VMEM((SUBLANE, c), F32),
                        pltpu.VMEM((ts, c), F32),
                        pltpu.VMEM((ts, c), F32)],
        compiler_params=_cparams(("arbitrary", "arbitrary")),
        name="rg_lru",
    )(proj, proj, cw, cb, wa, ba, wx, bx, lam, g)


def _compress_one(kv_ref, pe_ref, w1_ref, b1_ref, w2_ref):
    s, dh = kv_ref.shape
    n = s // CMP_STRIDE
    half = CMP_STRIDE * dh
    xa = jnp.concatenate([kv_ref[pl.ds(l, n, stride=CMP_STRIDE), :] for l in range(CMP_STRIDE)],
                         axis=1).astype(BF16)
    lo = jnp.dot(xa, w1_ref[0:half, :], preferred_element_type=F32)
    hi = jnp.dot(xa, w1_ref[half:2 * half, :], preferred_element_type=F32)
    pe = jnp.broadcast_to(pe_ref[...], (SUBLANE, 2 * half)).astype(BF16)
    pe_term = jnp.dot(pe, w1_ref[...], preferred_element_type=F32)[0:1, :]
    pre = lo + pltpu.roll(hi, n - 1, 0) + pe_term + b1_ref[...]
    return jnp.dot(jax.nn.gelu(pre).astype(BF16), w2_ref[...], preferred_element_type=F32)


def _compress_kernel(kc_ref, vc_ref, pek_ref, w1k_ref, b1k_ref, w2k_ref,
                     pev_ref, w1v_ref, b1v_ref, w2v_ref, ko_ref, vo_ref):
    ko_ref[0] = _compress_one(kc_ref, pek_ref, w1k_ref, b1k_ref, w2k_ref)
    vo_ref[0] = _compress_one(vc_ref, pev_ref, w1v_ref, b1v_ref, w2v_ref)


def _compress(proj, bsz, s, kc_blk, vc_blk, pk, pv):
    dh = HEAD_DIM
    n = s // CMP_STRIDE
    full = lambda a: pl.BlockSpec(a.shape, lambda b, h: (0,) * a.ndim)
    out = pl.BlockSpec((1, n, dh), lambda b, h: (b * N_KV_HEADS + h, 0, 0))
    shp = jax.ShapeDtypeStruct((bsz * N_KV_HEADS, n, dh), F32)
    return pl.pallas_call(
        _compress_kernel,
        grid=(bsz, N_KV_HEADS),
        in_specs=[pl.BlockSpec((s, dh), lambda b, h: (b, kc_blk + h)),
                  pl.BlockSpec((s, dh), lambda b, h: (b, vc_blk + h)),
                  *[full(a) for a in pk], *[full(a) for a in pv]],
        out_specs=[out, out],
        out_shape=[shp, shp],
        compiler_params=_cparams(("arbitrary", "arbitrary")),
        name="compress_kv",
    )(proj, proj, *pk, *pv)


KEY_CHUNK = 512
ROW_GROUPS = 2


def _attn_kernel(q_ref, ksel_ref, vsel_ref, kw_ref, vw_ref, gate_ref, kcmp_ref, vcmp_ref, ovt_ref,
                 o_ref, kaug_ref, vsaug_ref, kwb_ref, vwaug_ref, dbias_ref, wbias_ref,
                 qaug_ref, s_a, s_b, p_a, p_b, acc_ref, m_ref, a_ref):
    i = pl.program_id(2)
    s_len, dh = ksel_ref.shape
    tq = q_ref.shape[0]
    n_cmp = kcmp_ref.shape[1]
    n_sel = s_len // SEL_BLOCK
    gh = GQA // ROW_GROUPS
    rg = gh * tq

    @pl.when(i == 0)
    def _():
        kaug_ref[:, 0:dh] = ksel_ref[...].astype(BF16)
        kblk = lax.broadcasted_iota(jnp.int32, (s_len, LANE), 0) // SEL_BLOCK
        lane = lax.broadcasted_iota(jnp.int32, (s_len, LANE), 1)
        kaug_ref[:, dh:dh + LANE] = jnp.where(kblk == lane, 1.0, 0.0).astype(BF16)
        ones = jnp.ones((s_len, LANE), BF16)
        vsaug_ref[:, 0:dh] = vsel_ref[...].astype(BF16)
        vsaug_ref[:, dh:dh + LANE] = ones
        kwb_ref[...] = kw_ref[...].astype(BF16)
        vwaug_ref[:, 0:dh] = vw_ref[...].astype(BF16)
        vwaug_ref[:, dh:dh + LANE] = ones
        for v in range(dbias_ref.shape[0]):
            col = lax.broadcasted_iota(jnp.int32, (rg, KEY_CHUNK), 1)
            tok = lax.broadcasted_iota(jnp.int32, (rg, KEY_CHUNK), 0) & (tq - 1)
            dbias_ref[v] = jnp.where(col <= v * tq + tok, 0.0, NEG_INF)
        for v in range(wbias_ref.shape[0]):
            col = lax.broadcasted_iota(jnp.int32, wbias_ref.shape[1:], 1)
            tok = lax.broadcasted_iota(jnp.int32, wbias_ref.shape[1:], 0) & (tq - 1)
            d = v * tq + tok - col
            wbias_ref[v] = jnp.where(d >= 0, jnp.where(d < WINDOW, 0.0, NEG_INF), NEG_INF)

    nt = (((1,), (1,)), ((), ()))
    qs = i * tq
    qf = q_ref[...] * (HEAD_DIM ** -0.5)
    qb = [jnp.concatenate([qf[:, g * dh:(g + 1) * dh] for g in range(sp * gh, (sp + 1) * gh)],
                          axis=0).astype(BF16) for sp in range(ROW_GROUPS)]
    t_row = qs + (lax.broadcasted_iota(jnp.int32, (rg, 1), 0) & (tq - 1))

    wlen = wbias_ref.shape[2]
    w0 = pl.multiple_of(jnp.maximum(qs + tq - wlen, 0), tq)
    band_bias = wbias_ref[jnp.minimum(i, wbias_ref.shape[0] - 1)]
    kcb = kcmp_ref[0].astype(BF16)
    vcb = vcmp_ref[0].astype(BF16)
    vis_c = (lax.broadcasted_iota(jnp.int32, (rg, n_cmp), 1) * CMP_STRIDE + (CMP_BLOCK - 1)) <= t_row

    def cmp_scores(sp):
        return lax.dot_general(qb[sp], kcb, nt, preferred_element_type=F32)

    def cmp_probs(s_c):
        s_c = jnp.where(vis_c, s_c, NEG_INF)
        e_c = jnp.where(vis_c, jnp.exp(s_c - jnp.max(s_c, axis=1, keepdims=True)), 0.0)
        den_c = jnp.sum(e_c, axis=1, keepdims=True)
        return e_c / jnp.where(den_c > 0.0, den_c, 1.0)

    def win_scores(sp):
        return lax.dot_general(qb[sp], kwb_ref[pl.ds(w0, wlen), :], nt, preferred_element_type=F32)

    def win_probs(s_w):
        s_w = s_w + band_bias
        return jnp.exp(s_w - jnp.max(s_w, axis=1, keepdims=True)).astype(BF16)

    def win_out(e_w):
        pv = jnp.dot(e_w, vwaug_ref[pl.ds(w0, wlen), :], preferred_element_type=F32)
        return pv[:, :dh] / pv[:, dh:]

    s_c = [cmp_scores(sp) for sp in range(ROW_GROUPS)]
    s_w = [win_scores(sp) for sp in range(ROW_GROUPS)]
    p_c, o_c, o_w, e_w = [], [], [], []
    for sp in range(ROW_GROUPS):
        p_c.append(cmp_probs(s_c[sp]))
        o_c.append(jnp.dot(p_c[sp].astype(BF16), vcb, preferred_element_type=F32))
    p_sum = None
    for sp in range(ROW_GROUPS):
        for g in range(gh):
            pg = p_c[sp][g * tq:(g + 1) * tq]
            p_sum = pg if p_sum is None else p_sum + pg

    imp_t = lax.dot_general(ovt_ref[...], p_sum, nt, precision=lax.Precision.HIGHEST,
                            preferred_element_type=F32)
    blk = lax.broadcasted_iota(jnp.int32, (n_sel, tq), 0)
    cur = (qs + lax.broadcasted_iota(jnp.int32, (n_sel, tq), 1)) // SEL_BLOCK
    age = cur - blk
    score = jnp.where(blk == 0, FORCE_SCORE,
                      jnp.where(age < 0, -1.0, jnp.where(age < N_LOCAL_SEL, FORCE_SCORE, imp_t)))
    parts = [jnp.zeros((n_sel, tq), jnp.int32) for _ in range(4)]
    for j in range(n_sel):
        rj = score[j:j + 1, :]
        parts[j % 4] = parts[j % 4] + jnp.where(blk > j, jnp.where(rj >= score, 1, 0),
                                                jnp.where(rj > score, 1, 0))
    rank = (parts[0] + parts[1]) + (parts[2] + parts[3])
    bias_t = jnp.where(rank < min(N_SEL, n_sel), 0.0, NEG_INF)
    bias_t = jnp.concatenate([bias_t, jnp.zeros((LANE - n_sel, tq), F32)], axis=0)
    bias = jnp.concatenate([bias_t.T.astype(BF16)] * gh, axis=0)
    for sp in range(ROW_GROUPS):
        qaug_ref[sp] = jnp.concatenate([qb[sp], bias], axis=1)

    def scores(sp, k0):
        return lax.dot_general(qaug_ref[sp], kaug_ref[pl.ds(k0, KEY_CHUNK), :], nt,
                               preferred_element_type=F32)

    def accumulate(sp, p, alpha, k0):
        pv = jnp.dot(p, vsaug_ref[pl.ds(k0, KEY_CHUNK), :], preferred_element_type=F32)
        acc_ref[sp] = alpha * acc_ref[sp] + pv

    def soften(sp, s):
        m_old = m_ref[sp]
        m_new = jnp.maximum(m_old, jnp.max(s, axis=1, keepdims=True))
        m_ref[sp] = m_new
        return jnp.exp(m_old - m_new), jnp.exp(s - m_new).astype(BF16)

    def step(c, s_in, p_prev, s_out, p_out):
        k_next = pl.multiple_of((c + 1) * KEY_CHUNK, KEY_CHUNK)
        k_prev = pl.multiple_of(jnp.maximum(c - 1, 0) * KEY_CHUNK, KEY_CHUNK)
        for sp in range(ROW_GROUPS):
            accumulate(sp, p_prev[sp], a_ref[sp], k_prev)
            s_out[sp] = scores(sp, k_next)
        for sp in range(ROW_GROUPS):
            a_ref[sp], p_out[sp] = soften(sp, s_in[sp])

    def finish(n, s_in, p_prev):
        k_last = pl.multiple_of(jnp.maximum(n - 1, 0) * KEY_CHUNK, KEY_CHUNK)
        k_diag = pl.multiple_of(n * KEY_CHUNK, KEY_CHUNK)
        causal_bias = dbias_ref[i % (KEY_CHUNK // tq)]
        for sp in range(ROW_GROUPS):
            accumulate(sp, p_prev[sp], a_ref[sp], k_last)
            alpha, p = soften(sp, s_in[sp] + causal_bias)
            accumulate(sp, p, alpha, k_diag)

    n_full = qs // KEY_CHUNK
    for sp in range(ROW_GROUPS):
        s_a[sp] = scores(sp, 0)
        p_b[sp] = jnp.zeros((rg, KEY_CHUNK), BF16)
        a_ref[sp] = jnp.ones((rg, 1), F32)
        m_ref[sp] = jnp.full((rg, 1), NEG_INF, F32)
        acc_ref[sp] = jnp.zeros((rg, dh + LANE), F32)

    for sp in range(ROW_GROUPS):
        e_w.append(win_probs(s_w[sp]))
        o_w.append(win_out(e_w[sp]))
    gate = jax.nn.sigmoid(gate_ref[...])
    for g in range(GQA):
        sp, gg = divmod(g, gh)
        sl = slice(gg * tq, (gg + 1) * tq)
        o_ref[:, g * dh:(g + 1) * dh] = (gate[:, N_BRANCH * g:N_BRANCH * g + 1] * o_c[sp][sl]
                                         + gate[:, N_BRANCH * g + 2:N_BRANCH * g + 3] * o_w[sp][sl])

    def pair(k, carry):
        step(2 * k, s_a, p_b, s_b, p_a)
        step(2 * k + 1, s_b, p_a, s_a, p_b)
        return carry

    lax.fori_loop(0, n_full // 2, pair, 0)

    @pl.when(n_full % 2 == 0)
    def _():
        finish(n_full, s_a, p_b)

    @pl.when(n_full % 2 == 1)
    def _():
        step(n_full - 1, s_a, p_b, s_b, p_a)
        finish(n_full, s_b, p_a)

    gate = jax.nn.sigmoid(gate_ref[...])
    for g in range(GQA):
        sp, gg = divmod(g, gh)
        acc = acc_ref[sp, gg * tq:(gg + 1) * tq, :]
        o_ref[:, g * dh:(g + 1) * dh] += gate[:, N_BRANCH * g + 1:N_BRANCH * g + 2] * (acc[:, :dh] / acc[:, dh:])


def _attention(proj, gates, k_cmp, v_cmp, ov_t, bsz, s, blocks):
    dh = HEAD_DIM
    tq = Q_BLOCK
    nq = s // tq
    rg = (GQA // ROW_GROUPS) * tq
    n_cmp = k_cmp.shape[1]
    kv = lambda blk: pl.BlockSpec((s, dh), lambda b, h, i: (b, blk + h))
    cmp_spec = pl.BlockSpec((1, n_cmp, dh), lambda b, h, i: (b * N_KV_HEADS + h, 0, 0))
    return pl.pallas_call(
        _attn_kernel,
        grid=(bsz, N_KV_HEADS, nq),
        in_specs=[pl.BlockSpec((tq, GQA * dh), lambda b, h, i: (b * nq + i, blocks["q"] + h)),
                  kv(blocks["ksel"]), kv(blocks["vsel"]), kv(blocks["kw"]), kv(blocks["vw"]),
                  pl.BlockSpec((tq, LANE), lambda b, h, i: (b * nq + i, h)),
                  cmp_spec, cmp_spec,
                  pl.BlockSpec(ov_t.shape, lambda b, h, i: (0, 0))],
        out_specs=pl.BlockSpec((tq, GQA * dh), lambda b, h, i: (b * nq + i, h)),
        out_shape=jax.ShapeDtypeStruct((bsz * s, N_HEADS * dh), F32),
        scratch_shapes=[pltpu.VMEM((s, dh + LANE), BF16),
                        pltpu.VMEM((s, dh + LANE), BF16),
                        pltpu.VMEM((s, dh), BF16),
                        pltpu.VMEM((s, dh + LANE), BF16),
                        pltpu.VMEM((KEY_CHUNK // tq, rg, KEY_CHUNK), F32),
                        pltpu.VMEM((WINDOW // tq + 1, rg, min(WINDOW + tq, s)), F32),
                        pltpu.VMEM((ROW_GROUPS, rg, dh + LANE), BF16),
                        pltpu.VMEM((ROW_GROUPS, rg, KEY_CHUNK), F32),
                        pltpu.VMEM((ROW_GROUPS, rg, KEY_CHUNK), F32),
                        pltpu.VMEM((ROW_GROUPS, rg, KEY_CHUNK), BF16),
                        pltpu.VMEM((ROW_GROUPS, rg, KEY_CHUNK), BF16),
                        pltpu.VMEM((ROW_GROUPS, rg, dh + LANE), F32),
                        pltpu.VMEM((ROW_GROUPS, rg, 1), F32),
                        pltpu.VMEM((ROW_GROUPS, rg, 1), F32)],
        compiler_params=_cparams(("arbitrary", "arbitrary", "arbitrary")),
        name="nsa_attention",
    )(proj, proj, proj, proj, proj, gates, k_cmp, v_cmp, ov_t)


def _outproj_kernel(lru_ref, attn_ref, ga_ref, x_ref, w_ref, gf_ref, h_ref, xn_ref):
    c = lru_ref.shape[1]
    an = _rms(attn_ref[...], ga_ref[...]).astype(BF16)
    h = (x_ref[...]
         + jnp.dot(lru_ref[...], w_ref[0:c, :], preferred_element_type=F32)
         + jnp.dot(an, w_ref[c:, :], preferred_element_type=F32))
    h_ref[...] = h
    xn_ref[...] = _rms(h, gf_ref[...]).astype(BF16)


def _out_proj(lru_n, attn, g_attn, x2, w_out, g_ffn, tm):
    t, d = x2.shape
    c = lru_n.shape[1]
    return pl.pallas_call(
        _outproj_kernel,
        grid=(t // tm,),
        in_specs=[pl.BlockSpec((tm, c), lambda i: (i, 0)),
                  pl.BlockSpec((tm, attn.shape[1]), lambda i: (i, 0)),
                  pl.BlockSpec((1, attn.shape[1]), lambda i: (0, 0)),
                  pl.BlockSpec((tm, d), lambda i: (i, 0)),
                  pl.BlockSpec(w_out.shape, lambda i: (0, 0), pipeline_mode=pl.Buffered(1)),
                  pl.BlockSpec((1, d), lambda i: (0, 0))],
        out_specs=[pl.BlockSpec((tm, d), lambda i: (i, 0)),
                   pl.BlockSpec((tm, d), lambda i: (i, 0))],
        out_shape=[jax.ShapeDtypeStruct((t, d), F32), jax.ShapeDtypeStruct((t, d), BF16)],
        compiler_params=_cparams(("arbitrary",)),
        name="out_proj",
    )(lru_n, attn, g_attn, x2, w_out, g_ffn)


def _ffn_kernel(xn_ref, wu_ref, wv_ref, cw_ref, cb_ref, wd_ref, h_ref, g_ref, o_ref,
                halo_ref, upad_ref, *, tiles_per_seq, final_norm):
    i = pl.program_id(0)
    j = pl.program_id(1)
    tm = xn_ref.shape[0]

    @pl.when(j == 0)
    def _():
        o_ref[...] = h_ref[...]

    xn = xn_ref[...]
    u = jnp.dot(xn, wu_ref[...], preferred_element_type=F32)
    v = jnp.dot(xn, wv_ref[...], preferred_element_type=F32)
    first = (i % tiles_per_seq) == 0
    upad_ref[0:SUBLANE, :] = jnp.where(first, 0.0, halo_ref[j])
    upad_ref[SUBLANE:, :] = u
    halo_ref[j] = u[tm - SUBLANE:, :]
    uc = cb_ref[...] + cw_ref[FFN_CONV - 1:FFN_CONV, :] * u
    for k in range(FFN_CONV - 1):
        off = SUBLANE - (FFN_CONV - 1) + k
        uc = uc + cw_ref[k:k + 1, :] * upad_ref[off:off + tm, :]
    act = (jax.nn.gelu(uc) * v).astype(BF16)
    o_ref[...] += jnp.dot(act, wd_ref[...], preferred_element_type=F32)

    if final_norm:
        @pl.when(j == pl.num_programs(1) - 1)
        def _():
            o_ref[...] = _rms(o_ref[...], g_ref[...])


def _ffn(xn, h1, w_up, cw, cb, w_down, g_final, final_norm, s, tm, tf):
    t, d = xn.shape
    dff = w_down.shape[0]
    nj = dff // tf
    return pl.pallas_call(
        functools.partial(_ffn_kernel, tiles_per_seq=s // tm, final_norm=final_norm),
        grid=(t // tm, nj),
        in_specs=[pl.BlockSpec((tm, d), lambda i, j: (i, 0)),
                  pl.BlockSpec((d, tf), lambda i, j: (0, j)),
                  pl.BlockSpec((d, tf), lambda i, j: (0, nj + j)),
                  pl.BlockSpec((FFN_CONV, tf), lambda i, j: (0, j)),
                  pl.BlockSpec((1, tf), lambda i, j: (0, j)),
                  pl.BlockSpec((tf, d), lambda i, j: (j, 0)),
                  pl.BlockSpec((tm, d), lambda i, j: (i, 0)),
                  pl.BlockSpec((1, d), lambda i, j: (0, 0))],
        out_specs=pl.BlockSpec((tm, d), lambda i, j: (i, 0)),
        out_shape=jax.ShapeDtypeStruct((t, d), F32),
        scratch_shapes=[pltpu.VMEM((nj, SUBLANE, tf), F32),
                        pltpu.VMEM((tm + SUBLANE, tf), F32)],
        compiler_params=_cparams(("arbitrary", "arbitrary")),
        name="conv_ffn",
    )(xn, w_up, w_up, cw, cb, w_down, h1, g_final)


def _overlap_t(n_cmp, n_sel):
    c0 = jnp.arange(n_cmp)[None, :] * CMP_STRIDE
    s0 = jnp.arange(n_sel)[:, None] * SEL_BLOCK
    ov = jnp.clip(jnp.minimum(c0 + CMP_BLOCK, s0 + SEL_BLOCK) - jnp.maximum(c0, s0), 0)
    return ov.astype(F32) / CMP_BLOCK


def kernel(x, g_mix, w_in, lru_conv_w, lru_conv_b, lru_wa, lru_ba, lru_wx, lru_bx, lru_lambda, cmp_pe_k, cmp_w1_k, cmp_b1_k, cmp_w2_k, cmp_pe_v, cmp_w1_v, cmp_b1_v, cmp_w2_v, g_lru_out, g_attn_out, w_out, g_ffn, w_up, ffn_conv_w, ffn_conv_b, w_down, g_final):
    bsz, s, d = x.shape
    depth = w_in.shape[0]
    c_lru = lru_conv_w.shape[2]
    attn_w = N_HEADS * HEAD_DIM
    kv_w = N_KV_HEADS * HEAD_DIM
    n_gate = N_BRANCH * GQA
    main_w = 2 * c_lru + attn_w + 6 * kv_w
    kv0 = (2 * c_lru + attn_w) // HEAD_DIM
    blocks = {"q": (2 * c_lru) // (GQA * HEAD_DIM), "kc": kv0, "vc": kv0 + 2, "ksel": kv0 + 4,
              "vsel": kv0 + 6, "kw": kv0 + 8, "vw": kv0 + 10}
    ov_t = _overlap_t(s // CMP_STRIDE, s // SEL_BLOCK)
    row = lambda a: a.reshape(1, -1)

    h = x.reshape(bsz * s, d)
    for l in range(depth):
        w_main = w_in[l].astype(BF16)
        gpad = jnp.zeros((d, LANE - n_gate), F32)
        w_gate = jnp.concatenate([p for hh in range(N_KV_HEADS)
                                  for p in (w_in[l][:, main_w + hh * n_gate:main_w + (hh + 1) * n_gate], gpad)],
                                 axis=1).astype(BF16)
        proj, gates = _in_proj(h, row(g_mix[l]), w_main, main_w, w_gate, tm=512)
        lru_n = _lru(proj, bsz, s, lru_conv_w[l], row(lru_conv_b[l]), lru_wa[l].astype(BF16), row(lru_ba[l]),
                     lru_wx[l].astype(BF16), row(lru_bx[l]), row(lru_lambda[l]), row(g_lru_out[l]), ts=256)
        pk = (cmp_pe_k[l].reshape(1, -1), cmp_w1_k[l].astype(BF16), row(cmp_b1_k[l]), cmp_w2_k[l].astype(BF16))
        pv = (cmp_pe_v[l].reshape(1, -1), cmp_w1_v[l].astype(BF16), row(cmp_b1_v[l]), cmp_w2_v[l].astype(BF16))
        k_cmp, v_cmp = _compress(proj, bsz, s, blocks["kc"], blocks["vc"], pk, pv)
        attn = _attention(proj, gates, k_cmp, v_cmp, ov_t, bsz, s, blocks)
        h1, xn = _out_proj(lru_n, attn, row(g_attn_out[l]), h, w_out[l].astype(BF16), row(g_ffn[l]), tm=512)
        h = _ffn(xn, h1, w_up[l].astype(BF16), ffn_conv_w[l], row(ffn_conv_b[l]), w_down[l].astype(BF16),
                 row(g_final), l == depth - 1, s, tm=512, tf=1024)
    return h.reshape(bsz, s, d)
```

```python
import functools

import jax
import jax.numpy as jnp
from jax import lax
from jax.experimental import pallas as pl
from jax.experimental.pallas import tpu as pltpu

F32 = jnp.float32
BF16 = jnp.bfloat16

LANE = 128
SUBLANE = 8
VMEM_LIMIT = 56 * 1024 * 1024

LRU_HEADS = 8
LRU_CONV = 4
LRU_C = 8.0
N_HEADS = 8
N_KV_HEADS = 2
GQA = N_HEADS // N_KV_HEADS
HEAD_DIM = 128
N_BRANCH = 3
CMP_BLOCK = 32
CMP_STRIDE = 16
SEL_BLOCK = 64
N_SEL = 16
N_LOCAL_SEL = 2
WINDOW = 512
Q_BLOCK = 256
FFN_CONV = 3
EPS = 1e-6
NEG_INF = -1e30
FORCE_SCORE = 1e4


def _rms(x, g):
    return x * lax.rsqrt(jnp.mean(x * x, axis=-1, keepdims=True) + EPS) * g


def _cparams(sem):
    return pltpu.CompilerParams(dimension_semantics=sem, vmem_limit_bytes=VMEM_LIMIT)


def _inproj_kernel(x_ref, g_ref, w_ref, wg_ref, o_ref, og_ref):
    xn = _rms(x_ref[...], g_ref[...]).astype(BF16)
    o_ref[...] = jnp.dot(xn, w_ref[...], preferred_element_type=F32)
    og_ref[...] = jnp.dot(xn, wg_ref[...], preferred_element_type=F32)


def _in_proj(x2, g, w_all, n, w_gate, tm):
    t, d = x2.shape
    ng = w_gate.shape[1]
    resident = lambda shape: pl.BlockSpec(shape, lambda i: (0, 0), pipeline_mode=pl.Buffered(1))
    return pl.pallas_call(
        _inproj_kernel,
        grid=(t // tm,),
        in_specs=[pl.BlockSpec((tm, d), lambda i: (i, 0)),
                  pl.BlockSpec((1, d), lambda i: (0, 0)),
                  resident((d, n)), resident(w_gate.shape)],
        out_specs=[pl.BlockSpec((tm, n), lambda i: (i, 0)),
                   pl.BlockSpec((tm, ng), lambda i: (i, 0))],
        out_shape=[jax.ShapeDtypeStruct((t, n), F32), jax.ShapeDtypeStruct((t, ng), F32)],
        compiler_params=_cparams(("arbitrary",)),
        name="in_proj",
    )(x2, g, w_all, w_gate)


def _cast_rider_specs(arrays, n_steps, index):
    specs, shapes = [], []
    for a in arrays:
        rows = a.shape[0] // n_steps
        assert rows * n_steps == a.shape[0] and rows % (2 * SUBLANE) == 0
        specs.append(pl.BlockSpec((rows, a.shape[1]), index))
        shapes.append(jax.ShapeDtypeStruct(a.shape, BF16))
    return specs, shapes


def _lru_kernel(x_ref, gate_ref, cw_ref, cb_ref, wa_ref, ba_ref, wx_ref, bx_ref, lam_ref, g_ref, *rest, n_cast):
    cast_in, o_ref, cast_out = rest[:n_cast], rest[n_cast], rest[n_cast + 1:2 * n_cast + 1]
    xpad_ref, carry_ref, a_ref, b_ref = rest[2 * n_cast + 1:]
    for src, dst in zip(cast_in, cast_out):
        dst[...] = src[...].astype(dst.dtype)
    ts, c = x_ref.shape
    hb = c // LRU_HEADS

    @pl.when(pl.program_id(1) == 0)
    def _():
        xpad_ref[...] = jnp.zeros_like(xpad_ref)
        carry_ref[...] = jnp.zeros_like(carry_ref)

    xpad_ref[0:SUBLANE, :] = xpad_ref[ts:ts + SUBLANE, :]
    xpad_ref[SUBLANE:, :] = x_ref[...]
    u = cb_ref[...] + cw_ref[LRU_CONV - 1:LRU_CONV, :] * x_ref[...]
    for k in range(LRU_CONV - 1):
        off = SUBLANE - (LRU_CONV - 1) + k
        u = u + cw_ref[k:k + 1, :] * xpad_ref[off:off + ts, :]

    ub = u.astype(BF16)
    r_parts, i_parts = [], []
    for h in range(LRU_HEADS):
        uh = ub[:, h * hb:(h + 1) * hb]
        r_parts.append(jnp.dot(uh, wa_ref[h], preferred_element_type=F32))
        i_parts.append(jnp.dot(uh, wx_ref[h], preferred_element_type=F32))
    r = jax.nn.sigmoid(jnp.concatenate(r_parts, axis=1) + ba_ref[...])
    ig = jax.nn.sigmoid(jnp.concatenate(i_parts, axis=1) + bx_ref[...])
    nl = -lam_ref[...]
    softplus = jnp.maximum(nl, 0.0) + jnp.log1p(jnp.exp(-jnp.abs(nl)))
    log_a = (-LRU_C) * r * softplus
    a = jnp.exp(log_a)
    a_ref[...] = a
    b_ref[...] = jnp.sqrt(-jnp.tanh(log_a) * (a * a + 1.0)) * (ig * u)

    row = lax.broadcasted_iota(jnp.int32, (SUBLANE, c), 0)

    def group(gi, carry):
        r0 = pl.multiple_of(gi * SUBLANE, SUBLANE)
        a8 = a_ref[pl.ds(r0, SUBLANE), :]
        b8 = b_ref[pl.ds(r0, SUBLANE), :]
        for d in (1, 2, 4):
            a_sh = pltpu.roll(a8, d, 0)
            b_sh = pltpu.roll(b8, d, 0)
            m = row >= d
            b8 = jnp.where(m, a8 * b_sh + b8, b8)
            a8 = jnp.where(m, a8 * a_sh, a8)
        h8 = a8 * carry + b8
        b_ref[pl.ds(r0, SUBLANE), :] = h8
        return jnp.broadcast_to(h8[SUBLANE - 1:SUBLANE, :], (SUBLANE, c))

    carry_ref[...] = lax.fori_loop(0, ts // SUBLANE, group, carry_ref[...])

    y = b_ref[...] * jax.nn.gelu(gate_ref[...])
    o_ref[...] = _rms(y, g_ref[...]).astype(o_ref.dtype)


def _lru(proj, bsz, s, cw, cb, wa, ba, wx, bx, lam, g, ts, cast=()):
    c = cw.shape[1]
    ns = s // ts
    row = lambda b, i: b * ns + i
    vec = pl.BlockSpec((1, c), lambda b, i: (0, 0))
    mat = pl.BlockSpec(wa.shape, lambda b, i: (0, 0, 0))
    cast_specs, cast_shapes = _cast_rider_specs(cast, bsz * ns, lambda b, i: (row(b, i), 0))
    return pl.pallas_call(
        functools.partial(_lru_kernel, n_cast=len(cast)),
        grid=(bsz, ns),
        in_specs=[pl.BlockSpec((ts, c), lambda b, i: (row(b, i), 0)),
                  pl.BlockSpec((ts, c), lambda b, i: (row(b, i), 1)),
                  pl.BlockSpec((LRU_CONV, c), lambda b, i: (0, 0)), vec,
                  mat, vec, mat, vec, vec, vec, *cast_specs],
        out_specs=[pl.BlockSpec((ts, c), lambda b, i: (row(b, i), 0)), *cast_specs],
        out_shape=[jax.ShapeDtypeStruct((bsz * s, c), BF16), *cast_shapes],
        scratch_shapes=[pltpu.VMEM((ts + SUBLANE, c), F32),
                        pltpu.VMEM((SUBLANE, c), F32),
                        pltpu.VMEM((ts, c), F32),
                        pltpu.VMEM((ts, c), F32)],
        compiler_params=_cparams(("arbitrary", "arbitrary")),
        name="rg_lru",
    )(proj, proj, cw, cb, wa, ba, wx, bx, lam, g, *cast)


def _compress_one(kv_ref, pe_ref, w1_ref, b1_ref, w2_ref):
    s, dh = kv_ref.shape
    n = s // CMP_STRIDE
    half = CMP_STRIDE * dh
    xa = jnp.concatenate([kv_ref[pl.ds(l, n, stride=CMP_STRIDE), :] for l in range(CMP_STRIDE)],
                         axis=1).astype(BF16)
    lo = jnp.dot(xa, w1_ref[0:half, :], preferred_element_type=F32)
    hi = jnp.dot(xa, w1_ref[half:2 * half, :], preferred_element_type=F32)
    pe = jnp.broadcast_to(pe_ref[...], (SUBLANE, 2 * half)).astype(BF16)
    pe_term = jnp.dot(pe, w1_ref[...], preferred_element_type=F32)[0:1, :]
    pre = lo + pltpu.roll(hi, n - 1, 0) + pe_term + b1_ref[...]
    return jnp.dot(jax.nn.gelu(pre).astype(BF16), w2_ref[...], preferred_element_type=F32)


def _compress_kernel(kc_ref, vc_ref, pek_ref, w1k_ref, b1k_ref, w2k_ref,
                     pev_ref, w1v_ref, b1v_ref, w2v_ref, ko_ref, vo_ref):
    ko_ref[0] = _compress_one(kc_ref, pek_ref, w1k_ref, b1k_ref, w2k_ref)
    vo_ref[0] = _compress_one(vc_ref, pev_ref, w1v_ref, b1v_ref, w2v_ref)


def _compress(proj, bsz, s, kc_blk, vc_blk, pk, pv):
    dh = HEAD_DIM
    n = s // CMP_STRIDE
    full = lambda a: pl.BlockSpec(a.shape, lambda b, h: (0,) * a.ndim)
    out = pl.BlockSpec((1, n, dh), lambda b, h: (b * N_KV_HEADS + h, 0, 0))
    shp = jax.ShapeDtypeStruct((bsz * N_KV_HEADS, n, dh), F32)
    return pl.pallas_call(
        _compress_kernel,
        grid=(bsz, N_KV_HEADS),
        in_specs=[pl.BlockSpec((s, dh), lambda b, h: (b, kc_blk + h)),
                  pl.BlockSpec((s, dh), lambda b, h: (b, vc_blk + h)),
                  *[full(a) for a in pk], *[full(a) for a in pv]],
        out_specs=[out, out],
        out_shape=[shp, shp],
        compiler_params=_cparams(("arbitrary", "arbitrary")),
        name="compress_kv",
    )(proj, proj, *pk, *pv)


KEY_CHUNK = 512
ROW_GROUPS = 2


def _attn_kernel(q_ref, ksel_ref, vsel_ref, kw_ref, vw_ref, gate_ref, kcmp_ref, vcmp_ref, ovt_ref,
                 o_ref, kaug_ref, vsaug_ref, kwb_ref, vwaug_ref, dbias_ref, wbias_ref,
                 qaug_ref, s_a, s_b, p_a, p_b, acc_ref, m_ref, a_ref):
    i = pl.program_id(2)
    s_len, dh = ksel_ref.shape
    tq = q_ref.shape[0]
    n_cmp = kcmp_ref.shape[1]
    n_sel = s_len // SEL_BLOCK
    gh = GQA // ROW_GROUPS
    rg = gh * tq

    @pl.when(i == 0)
    def _():
        kaug_ref[:, 0:dh] = ksel_ref[...].astype(BF16)
        kblk = lax.broadcasted_iota(jnp.int32, (s_len, LANE), 0) // SEL_BLOCK
        lane = lax.broadcasted_iota(jnp.int32, (s_len, LANE), 1)
        kaug_ref[:, dh:dh + LANE] = jnp.where(kblk == lane, 1.0, 0.0).astype(BF16)
        ones = jnp.ones((s_len, LANE), BF16)
        vsaug_ref[:, 0:dh] = vsel_ref[...].astype(BF16)
        vsaug_ref[:, dh:dh + LANE] = ones
        kwb_ref[...] = kw_ref[...].astype(BF16)
        vwaug_ref[:, 0:dh] = vw_ref[...].astype(BF16)
        vwaug_ref[:, dh:dh + LANE] = ones
        for v in range(dbias_ref.shape[0]):
            col = lax.broadcasted_iota(jnp.int32, (rg, KEY_CHUNK), 1)
            tok = lax.broadcasted_iota(jnp.int32, (rg, KEY_CHUNK), 0) & (tq - 1)
            dbias_ref[v] = jnp.where(col <= v * tq + tok, 0.0, NEG_INF)
        for v in range(wbias_ref.shape[0]):
            col = lax.broadcasted_iota(jnp.int32, wbias_ref.shape[1:], 1)
            tok = lax.broadcasted_iota(jnp.int32, wbias_ref.shape[1:], 0) & (tq - 1)
            d = v * tq + tok - col
            wbias_ref[v] = jnp.where(d >= 0, jnp.where(d < WINDOW, 0.0, NEG_INF), NEG_INF)

    nt = (((1,), (1,)), ((), ()))
    qs = i * tq
    qf = q_ref[...] * (HEAD_DIM ** -0.5)
    qb = [jnp.concatenate([qf[:, g * dh:(g + 1) * dh] for g in range(sp * gh, (sp + 1) * gh)],
                          axis=0).astype(BF16) for sp in range(ROW_GROUPS)]
    t_row = qs + (lax.broadcasted_iota(jnp.int32, (rg, 1), 0) & (tq - 1))

    wlen = wbias_ref.shape[2]
    w0 = pl.multiple_of(jnp.maximum(qs + tq - wlen, 0), tq)
    band_bias = wbias_ref[jnp.minimum(i, wbias_ref.shape[0] - 1)]
    kcb = kcmp_ref[0].astype(BF16)
    vcb = vcmp_ref[0].astype(BF16)
    vis_c = (lax.broadcasted_iota(jnp.int32, (rg, n_cmp), 1) * CMP_STRIDE + (CMP_BLOCK - 1)) <= t_row

    def cmp_scores(sp):
        return lax.dot_general(qb[sp], kcb, nt, preferred_element_type=F32)

    def cmp_probs(s_c):
        s_c = jnp.where(vis_c, s_c, NEG_INF)
        e_c = jnp.where(vis_c, jnp.exp(s_c - jnp.max(s_c, axis=1, keepdims=True)), 0.0)
        den_c = jnp.sum(e_c, axis=1, keepdims=True)
        return e_c / jnp.where(den_c > 0.0, den_c, 1.0)

    def win_scores(sp):
        return lax.dot_general(qb[sp], kwb_ref[pl.ds(w0, wlen), :], nt, preferred_element_type=F32)

    def win_probs(s_w):
        s_w = s_w + band_bias
        return jnp.exp(s_w - jnp.max(s_w, axis=1, keepdims=True)).astype(BF16)

    def win_out(e_w):
        pv = jnp.dot(e_w, vwaug_ref[pl.ds(w0, wlen), :], preferred_element_type=F32)
        return pv[:, :dh] / pv[:, dh:]

    s_c = [cmp_scores(sp) for sp in range(ROW_GROUPS)]
    s_w = [win_scores(sp) for sp in range(ROW_GROUPS)]
    p_c, o_c, o_w, e_w = [], [], [], []
    for sp in range(ROW_GROUPS):
        p_c.append(cmp_probs(s_c[sp]))
        o_c.append(jnp.dot(p_c[sp].astype(BF16), vcb, preferred_element_type=F32))
    p_sum = None
    for sp in range(ROW_GROUPS):
        for g in range(gh):
            pg = p_c[sp][g * tq:(g + 1) * tq]
            p_sum = pg if p_sum is None else p_sum + pg

    imp_t = lax.dot_general(ovt_ref[...], p_sum, nt, precision=lax.Precision.HIGHEST,
                            preferred_element_type=F32)
    blk = lax.broadcasted_iota(jnp.int32, (n_sel, tq), 0)
    cur = (qs + lax.broadcasted_iota(jnp.int32, (n_sel, tq), 1)) // SEL_BLOCK
    age = cur - blk
    score = jnp.where(blk == 0, FORCE_SCORE,
                      jnp.where(age < 0, -1.0, jnp.where(age < N_LOCAL_SEL, FORCE_SCORE, imp_t)))
    parts = [jnp.zeros((n_sel, tq), jnp.int32) for _ in range(4)]
    for j in range(n_sel):
        rj = score[j:j + 1, :]
        parts[j % 4] = parts[j % 4] + jnp.where(blk > j, jnp.where(rj >= score, 1, 0),
                                                jnp.where(rj > score, 1, 0))
    rank = (parts[0] + parts[1]) + (parts[2] + parts[3])
    bias_t = jnp.where(rank < min(N_SEL, n_sel), 0.0, NEG_INF)
    bias_t = jnp.concatenate([bias_t, jnp.zeros((LANE - n_sel, tq), F32)], axis=0)
    bias = jnp.concatenate([bias_t.T.astype(BF16)] * gh, axis=0)
    for sp in range(ROW_GROUPS):
        qaug_ref[sp] = jnp.concatenate([qb[sp], bias], axis=1)

    def scores(sp, k0):
        return lax.dot_general(qaug_ref[sp], kaug_ref[pl.ds(k0, KEY_CHUNK), :], nt,
                               preferred_element_type=F32)

    def accumulate(sp, p, alpha, k0):
        pv = jnp.dot(p, vsaug_ref[pl.ds(k0, KEY_CHUNK), :], preferred_element_type=F32)
        acc_ref[sp] = alpha * acc_ref[sp] + pv

    def soften(sp, s):
        m_old = m_ref[sp]
        m_new = jnp.maximum(m_old, jnp.max(s, axis=1, keepdims=True))
        m_ref[sp] = m_new
        return jnp.exp(m_old - m_new), jnp.exp(s - m_new).astype(BF16)

    n_full = qs // KEY_CHUNK
    k_diag = pl.multiple_of(n_full * KEY_CHUNK, KEY_CHUNK)

    def step(c, s_in, p_prev, s_out, p_out):
        k_next = pl.multiple_of((c + 1) * KEY_CHUNK, KEY_CHUNK)
        k_prev = pl.multiple_of(jnp.where(c == 0, n_full, c - 1) * KEY_CHUNK, KEY_CHUNK)
        for sp in range(ROW_GROUPS):
            accumulate(sp, p_prev[sp], a_ref[sp], k_prev)
            if s_out is not None:
                s_out[sp] = scores(sp, k_next)
        for sp in range(ROW_GROUPS):
            a_ref[sp], p_out[sp] = soften(sp, s_in[sp])

    causal_bias = dbias_ref[i % (KEY_CHUNK // tq)]
    for sp in range(ROW_GROUPS):
        m_ref[sp] = jnp.full((rg, 1), NEG_INF, F32)
        acc_ref[sp] = jnp.zeros((rg, dh + LANE), F32)
        a_ref[sp], p_b[sp] = soften(sp, scores(sp, k_diag) + causal_bias)
        s_a[sp] = scores(sp, 0)

    for sp in range(ROW_GROUPS):
        e_w.append(win_probs(s_w[sp]))
        o_w.append(win_out(e_w[sp]))
    gate = jax.nn.sigmoid(gate_ref[...])
    for g in range(GQA):
        sp, gg = divmod(g, gh)
        sl = slice(gg * tq, (gg + 1) * tq)
        o_ref[:, g * dh:(g + 1) * dh] = (gate[:, N_BRANCH * g:N_BRANCH * g + 1] * o_c[sp][sl]
                                         + gate[:, N_BRANCH * g + 2:N_BRANCH * g + 3] * o_w[sp][sl])

    def pair(k, carry):
        step(2 * k, s_a, p_b, s_b, p_a)
        step(2 * k + 1, s_b, p_a, s_a, p_b)
        return carry

    n_steps = jnp.maximum(n_full - 1, 0)
    lax.fori_loop(0, n_steps // 2, pair, 0)
    k_last = pl.multiple_of(n_steps * KEY_CHUNK, KEY_CHUNK)

    @pl.when(n_full == 0)
    def _():
        for sp in range(ROW_GROUPS):
            accumulate(sp, p_b[sp], a_ref[sp], k_diag)

    @pl.when((n_full > 0) & (n_steps % 2 == 0))
    def _():
        step(n_steps, s_a, p_b, None, p_a)
        for sp in range(ROW_GROUPS):
            accumulate(sp, p_a[sp], a_ref[sp], k_last)

    @pl.when((n_full > 0) & (n_steps % 2 == 1))
    def _():
        step(n_steps - 1, s_a, p_b, s_b, p_a)
        step(n_steps, s_b, p_a, None, p_b)
        for sp in range(ROW_GROUPS):
            accumulate(sp, p_b[sp], a_ref[sp], k_last)

    gate = jax.nn.sigmoid(gate_ref[...])
    for g in range(GQA):
        sp, gg = divmod(g, gh)
        acc = acc_ref[sp, gg * tq:(gg + 1) * tq, :]
        o_ref[:, g * dh:(g + 1) * dh] += gate[:, N_BRANCH * g + 1:N_BRANCH * g + 2] * (acc[:, :dh] / acc[:, dh:])


def _attention(proj, gates, k_cmp, v_cmp, ov_t, bsz, s, blocks):
    dh = HEAD_DIM
    tq = Q_BLOCK
    nq = s // tq
    rg = (GQA // ROW_GROUPS) * tq
    n_cmp = k_cmp.shape[1]
    kv = lambda blk: pl.BlockSpec((s, dh), lambda b, h, i: (b, blk + h))
    cmp_spec = pl.BlockSpec((1, n_cmp, dh), lambda b, h, i: (b * N_KV_HEADS + h, 0, 0))
    return pl.pallas_call(
        _attn_kernel,
        grid=(bsz, N_KV_HEADS, nq),
        in_specs=[pl.BlockSpec((tq, GQA * dh), lambda b, h, i: (b * nq + i, blocks["q"] + h)),
                  kv(blocks["ksel"]), kv(blocks["vsel"]), kv(blocks["kw"]), kv(blocks["vw"]),
                  pl.BlockSpec((tq, LANE), lambda b, h, i: (b * nq + i, h)),
                  cmp_spec, cmp_spec,
                  pl.BlockSpec(ov_t.shape, lambda b, h, i: (0, 0))],
        out_specs=pl.BlockSpec((tq, GQA * dh), lambda b, h, i: (b * nq + i, h)),
        out_shape=jax.ShapeDtypeStruct((bsz * s, N_HEADS * dh), F32),
        scratch_shapes=[pltpu.VMEM((s, dh + LANE), BF16),
                        pltpu.VMEM((s, dh + LANE), BF16),
                        pltpu.VMEM((s, dh), BF16),
                        pltpu.VMEM((s, dh + LANE), BF16),
                        pltpu.VMEM((KEY_CHUNK // tq, rg, KEY_CHUNK), F32),
                        pltpu.VMEM((WINDOW // tq + 1, rg, min(WINDOW + tq, s)), F32),
                        pltpu.VMEM((ROW_GROUPS, rg, dh + LANE), BF16),
                        pltpu.VMEM((ROW_GROUPS, rg, KEY_CHUNK), F32),
                        pltpu.VMEM((ROW_GROUPS, rg, KEY_CHUNK), F32),
                        pltpu.VMEM((ROW_GROUPS, rg, KEY_CHUNK), BF16),
                        pltpu.VMEM((ROW_GROUPS, rg, KEY_CHUNK), BF16),
                        pltpu.VMEM((ROW_GROUPS, rg, dh + LANE), F32),
                        pltpu.VMEM((ROW_GROUPS, rg, 1), F32),
                        pltpu.VMEM((ROW_GROUPS, rg, 1), F32)],
        compiler_params=_cparams(("arbitrary", "arbitrary", "arbitrary")),
        name="nsa_attention",
    )(proj, proj, proj, proj, proj, gates, k_cmp, v_cmp, ov_t)


def _outproj_kernel(lru_ref, attn_ref, ga_ref, x_ref, w_ref, gf_ref, *rest, n_cast):
    cast_in, (h_ref, xn_ref), cast_out = rest[:n_cast], rest[n_cast:n_cast + 2], rest[n_cast + 2:]
    for src, dst in zip(cast_in, cast_out):
        dst[...] = src[...].astype(dst.dtype)
    c = lru_ref.shape[1]
    an = _rms(attn_ref[...], ga_ref[...]).astype(BF16)
    h = (x_ref[...]
         + jnp.dot(lru_ref[...], w_ref[0:c, :], preferred_element_type=F32)
         + jnp.dot(an, w_ref[c:, :], preferred_element_type=F32))
    h_ref[...] = h
    xn_ref[...] = _rms(h, gf_ref[...]).astype(BF16)


def _out_proj(lru_n, attn, g_attn, x2, w_out, g_ffn, tm, cast=()):
    t, d = x2.shape
    c = lru_n.shape[1]
    cast_specs, cast_shapes = _cast_rider_specs(cast, t // tm, lambda i: (i, 0))
    return pl.pallas_call(
        functools.partial(_outproj_kernel, n_cast=len(cast)),
        grid=(t // tm,),
        in_specs=[pl.BlockSpec((tm, c), lambda i: (i, 0)),
                  pl.BlockSpec((tm, attn.shape[1]), lambda i: (i, 0)),
                  pl.BlockSpec((1, attn.shape[1]), lambda i: (0, 0)),
                  pl.BlockSpec((tm, d), lambda i: (i, 0)),
                  pl.BlockSpec(w_out.shape, lambda i: (0, 0), pipeline_mode=pl.Buffered(1)),
                  pl.BlockSpec((1, d), lambda i: (0, 0)), *cast_specs],
        out_specs=[pl.BlockSpec((tm, d), lambda i: (i, 0)),
                   pl.BlockSpec((tm, d), lambda i: (i, 0)), *cast_specs],
        out_shape=[jax.ShapeDtypeStruct((t, d), F32), jax.ShapeDtypeStruct((t, d), BF16), *cast_shapes],
        compiler_params=_cparams(("arbitrary",)),
        name="out_proj",
    )(lru_n, attn, g_attn, x2, w_out, g_ffn, *cast)


def _ffn_kernel(xn_ref, wu_ref, wv_ref, cw_ref, cb_ref, wd_ref, h_ref, g_ref, o_ref,
                halo_ref, upad_ref, *, tiles_per_seq, final_norm):
    i = pl.program_id(0)
    j = pl.program_id(1)
    tm = xn_ref.shape[0]

    @pl.when(j == 0)
    def _():
        o_ref[...] = h_ref[...]

    xn = xn_ref[...]
    u = jnp.dot(xn, wu_ref[...], preferred_element_type=F32)
    v = jnp.dot(xn, wv_ref[...], preferred_element_type=F32)
    first = (i % tiles_per_seq) == 0
    upad_ref[0:SUBLANE, :] = jnp.where(first, 0.0, halo_ref[j])
    upad_ref[SUBLANE:, :] = u
    halo_ref[j] = u[tm - SUBLANE:, :]
    uc = cb_ref[...] + cw_ref[FFN_CONV - 1:FFN_CONV, :] * u
    for k in range(FFN_CONV - 1):
        off = SUBLANE - (FFN_CONV - 1) + k
        uc = uc + cw_ref[k:k + 1, :] * upad_ref[off:off + tm, :]
    act = (jax.nn.gelu(uc) * v).astype(BF16)
    o_ref[...] += jnp.dot(act, wd_ref[...], preferred_element_type=F32)

    if final_norm:
        @pl.when(j == pl.num_programs(1) - 1)
        def _():
            o_ref[...] = _rms(o_ref[...], g_ref[...])


def _ffn(xn, h1, w_up, cw, cb, w_down, g_final, final_norm, s, tm, tf):
    t, d = xn.shape
    dff = w_down.shape[0]
    nj = dff // tf
    return pl.pallas_call(
        functools.partial(_ffn_kernel, tiles_per_seq=s // tm, final_norm=final_norm),
        grid=(t // tm, nj),
        in_specs=[pl.BlockSpec((tm, d), lambda i, j: (i, 0)),
                  pl.BlockSpec((d, tf), lambda i, j: (0, j)),
                  pl.BlockSpec((d, tf), lambda i, j: (0, nj + j)),
                  pl.BlockSpec((FFN_CONV, tf), lambda i, j: (0, j)),
                  pl.BlockSpec((1, tf), lambda i, j: (0, j)),
                  pl.BlockSpec((tf, d), lambda i, j: (j, 0)),
                  pl.BlockSpec((tm, d), lambda i, j: (i, 0)),
                  pl.BlockSpec((1, d), lambda i, j: (0, 0))],
        out_specs=pl.BlockSpec((tm, d), lambda i, j: (i, 0)),
        out_shape=jax.ShapeDtypeStruct((t, d), F32),
        scratch_shapes=[pltpu.VMEM((nj, SUBLANE, tf), F32),
                        pltpu.VMEM((tm + SUBLANE, tf), F32)],
        compiler_params=_cparams(("arbitrary", "arbitrary")),
        name="conv_ffn",
    )(xn, w_up, w_up, cw, cb, w_down, h1, g_final)


def _overlap_t(n_cmp, n_sel):
    c0 = jnp.arange(n_cmp)[None, :] * CMP_STRIDE
    s0 = jnp.arange(n_sel)[:, None] * SEL_BLOCK
    ov = jnp.clip(jnp.minimum(c0 + CMP_BLOCK, s0 + SEL_BLOCK) - jnp.maximum(c0, s0), 0)
    return ov.astype(F32) / CMP_BLOCK


def kernel(x, g_mix, w_in, lru_conv_w, lru_conv_b, lru_wa, lru_ba, lru_wx, lru_bx, lru_lambda, cmp_pe_k, cmp_w1_k, cmp_b1_k, cmp_w2_k, cmp_pe_v, cmp_w1_v, cmp_b1_v, cmp_w2_v, g_lru_out, g_attn_out, w_out, g_ffn, w_up, ffn_conv_w, ffn_conv_b, w_down, g_final):
    bsz, s, d = x.shape
    depth = w_in.shape[0]
    c_lru = lru_conv_w.shape[2]
    attn_w = N_HEADS * HEAD_DIM
    kv_w = N_KV_HEADS * HEAD_DIM
    n_gate = N_BRANCH * GQA
    main_w = 2 * c_lru + attn_w + 6 * kv_w
    kv0 = (2 * c_lru + attn_w) // HEAD_DIM
    blocks = {"q": (2 * c_lru) // (GQA * HEAD_DIM), "kc": kv0, "vc": kv0 + 2, "ksel": kv0 + 4,
              "vsel": kv0 + 6, "kw": kv0 + 8, "vw": kv0 + 10}
    ov_t = _overlap_t(s // CMP_STRIDE, s // SEL_BLOCK)
    row = lambda a: a.reshape(1, -1)

    h = x.reshape(bsz * s, d)
    for l in range(depth):
        w_main = w_in[l].astype(BF16)
        gpad = jnp.zeros((d, LANE - n_gate), F32)
        w_gate = jnp.concatenate([p for hh in range(N_KV_HEADS)
                                  for p in (w_in[l][:, main_w + hh * n_gate:main_w + (hh + 1) * n_gate], gpad)],
                                 axis=1).astype(BF16)
        proj, gates = _in_proj(h, row(g_mix[l]), w_main, main_w, w_gate, tm=512)
        lru_n, w_up_b, w_out_b = _lru(proj, bsz, s, lru_conv_w[l], row(lru_conv_b[l]), lru_wa[l].astype(BF16),
                                      row(lru_ba[l]), lru_wx[l].astype(BF16), row(lru_bx[l]), row(lru_lambda[l]),
                                      row(g_lru_out[l]), ts=256, cast=(w_up[l], w_out[l]))
        pk = (cmp_pe_k[l].reshape(1, -1), cmp_w1_k[l].astype(BF16), row(cmp_b1_k[l]), cmp_w2_k[l].astype(BF16))
        pv = (cmp_pe_v[l].reshape(1, -1), cmp_w1_v[l].astype(BF16), row(cmp_b1_v[l]), cmp_w2_v[l].astype(BF16))
        k_cmp, v_cmp = _compress(proj, bsz, s, blocks["kc"], blocks["vc"], pk, pv)
        attn = _attention(proj, gates, k_cmp, v_cmp, ov_t, bsz, s, blocks)
        h1, xn, w_down_b = _out_proj(lru_n, attn, row(g_attn_out[l]), h, w_out_b, row(g_ffn[l]), tm=512,
                                     cast=(w_down[l],))
        h = _ffn(xn, h1, w_up_b, ffn_conv_w[l], row(ffn_conv_b[l]), w_down_b,
                 row(g_final), l == depth - 1, s, tm=512, tf=1024)
    return h.reshape(bsz, s, d)
```

```python
import functools

import jax
import jax.numpy as jnp
from jax import lax
from jax.experimental import pallas as pl
from jax.experimental.pallas import tpu as pltpu

F32 = jnp.float32
BF16 = jnp.bfloat16

LANE = 128
SUBLANE = 8
VMEM_LIMIT = 56 * 1024 * 1024

LRU_HEADS = 8
LRU_CONV = 4
LRU_C = 8.0
N_HEADS = 8
N_KV_HEADS = 2
GQA = N_HEADS // N_KV_HEADS
HEAD_DIM = 128
N_BRANCH = 3
CMP_BLOCK = 32
CMP_STRIDE = 16
SEL_BLOCK = 64
N_SEL = 16
N_LOCAL_SEL = 2
WINDOW = 512
Q_BLOCK = 256
FFN_CONV = 3
EPS = 1e-6
NEG_INF = -1e30
FORCE_SCORE = 1e4


def _rms(x, g):
    return x * lax.rsqrt(jnp.mean(x * x, axis=-1, keepdims=True) + EPS) * g


def _cparams(sem):
    return pltpu.CompilerParams(dimension_semantics=sem, vmem_limit_bytes=VMEM_LIMIT)


def _inproj_kernel(x_ref, g_ref, w_ref, wg_ref, *o_refs, segments):
    xn = _rms(x_ref[...], g_ref[...]).astype(BF16)
    for (c0, width, scale), o_ref in zip(segments, o_refs):
        y = jnp.dot(xn, w_ref[:, c0:c0 + width], preferred_element_type=F32)
        o_ref[...] = (y if scale is None else y * scale).astype(o_ref.dtype)
    o_refs[-1][...] = jnp.dot(xn, wg_ref[...], preferred_element_type=F32)


def _in_proj(x2, g, w_all, segments, w_gate, tm):
    t, d = x2.shape
    ng = w_gate.shape[1]
    n = max(c0 + width for c0, width, _, _ in segments)
    resident = lambda shape: pl.BlockSpec(shape, lambda i: (0, 0), pipeline_mode=pl.Buffered(1))
    return pl.pallas_call(
        functools.partial(_inproj_kernel, segments=tuple((c0, width, sc) for c0, width, _, sc in segments)),
        grid=(t // tm,),
        in_specs=[pl.BlockSpec((tm, d), lambda i: (i, 0)),
                  pl.BlockSpec((1, d), lambda i: (0, 0)),
                  resident((d, n)), resident(w_gate.shape)],
        out_specs=[*[pl.BlockSpec((tm, width), lambda i: (i, 0)) for _, width, _, _ in segments],
                   pl.BlockSpec((tm, ng), lambda i: (i, 0))],
        out_shape=[*[jax.ShapeDtypeStruct((t, width), dt) for _, width, dt, _ in segments],
                   jax.ShapeDtypeStruct((t, ng), F32)],
        compiler_params=_cparams(("arbitrary",)),
        name="in_proj",
    )(x2, g, w_all, w_gate)


def _cast_rider_specs(arrays, n_steps, index):
    specs, shapes = [], []
    for a in arrays:
        rows = a.shape[0] // n_steps
        assert rows * n_steps == a.shape[0] and rows % (2 * SUBLANE) == 0
        specs.append(pl.BlockSpec((rows, a.shape[1]), index))
        shapes.append(jax.ShapeDtypeStruct(a.shape, BF16))
    return specs, shapes


def _lru_kernel(x_ref, gate_ref, cw_ref, cb_ref, wa_ref, ba_ref, wx_ref, bx_ref, lam_ref, g_ref, *rest, n_cast):
    cast_in, o_ref, cast_out = rest[:n_cast], rest[n_cast], rest[n_cast + 1:2 * n_cast + 1]
    xpad_ref, carry_ref, a_ref, b_ref = rest[2 * n_cast + 1:]
    for src, dst in zip(cast_in, cast_out):
        dst[...] = src[...].astype(dst.dtype)
    ts, c = x_ref.shape
    hb = c // LRU_HEADS

    @pl.when(pl.program_id(1) == 0)
    def _():
        xpad_ref[...] = jnp.zeros_like(xpad_ref)
        carry_ref[...] = jnp.zeros_like(carry_ref)

    xpad_ref[0:SUBLANE, :] = xpad_ref[ts:ts + SUBLANE, :]
    xpad_ref[SUBLANE:, :] = x_ref[...]
    u = cb_ref[...] + cw_ref[LRU_CONV - 1:LRU_CONV, :] * x_ref[...]
    for k in range(LRU_CONV - 1):
        off = SUBLANE - (LRU_CONV - 1) + k
        u = u + cw_ref[k:k + 1, :] * xpad_ref[off:off + ts, :]

    ub = u.astype(BF16)
    r_parts, i_parts = [], []
    for h in range(LRU_HEADS):
        uh = ub[:, h * hb:(h + 1) * hb]
        r_parts.append(jnp.dot(uh, wa_ref[h], preferred_element_type=F32))
        i_parts.append(jnp.dot(uh, wx_ref[h], preferred_element_type=F32))
    r = jax.nn.sigmoid(jnp.concatenate(r_parts, axis=1) + ba_ref[...])
    ig = jax.nn.sigmoid(jnp.concatenate(i_parts, axis=1) + bx_ref[...])
    nl = -lam_ref[...]
    softplus = jnp.maximum(nl, 0.0) + jnp.log1p(jnp.exp(-jnp.abs(nl)))
    log_a = (-LRU_C) * r * softplus
    a = jnp.exp(log_a)
    a_ref[...] = a
    b_ref[...] = jnp.sqrt(-jnp.tanh(log_a) * (a * a + 1.0)) * (ig * u)

    row = lax.broadcasted_iota(jnp.int32, (SUBLANE, c), 0)

    def group(gi, carry):
        r0 = pl.multiple_of(gi * SUBLANE, SUBLANE)
        a8 = a_ref[pl.ds(r0, SUBLANE), :]
        b8 = b_ref[pl.ds(r0, SUBLANE), :]
        for d in (1, 2, 4):
            a_sh = pltpu.roll(a8, d, 0)
            b_sh = pltpu.roll(b8, d, 0)
            m = row >= d
            b8 = jnp.where(m, a8 * b_sh + b8, b8)
            a8 = jnp.where(m, a8 * a_sh, a8)
        h8 = a8 * carry + b8
        b_ref[pl.ds(r0, SUBLANE), :] = h8
        return jnp.broadcast_to(h8[SUBLANE - 1:SUBLANE, :], (SUBLANE, c))

    carry_ref[...] = lax.fori_loop(0, ts // SUBLANE, group, carry_ref[...])

    y = b_ref[...] * jax.nn.gelu(gate_ref[...])
    o_ref[...] = _rms(y, g_ref[...]).astype(o_ref.dtype)


def _lru(proj, bsz, s, cw, cb, wa, ba, wx, bx, lam, g, ts, cast=()):
    c = cw.shape[1]
    ns = s // ts
    row = lambda b, i: b * ns + i
    vec = pl.BlockSpec((1, c), lambda b, i: (0, 0))
    mat = pl.BlockSpec(wa.shape, lambda b, i: (0, 0, 0))
    cast_specs, cast_shapes = _cast_rider_specs(cast, bsz * ns, lambda b, i: (row(b, i), 0))
    return pl.pallas_call(
        functools.partial(_lru_kernel, n_cast=len(cast)),
        grid=(bsz, ns),
        in_specs=[pl.BlockSpec((ts, c), lambda b, i: (row(b, i), 0)),
                  pl.BlockSpec((ts, c), lambda b, i: (row(b, i), 1)),
                  pl.BlockSpec((LRU_CONV, c), lambda b, i: (0, 0)), vec,
                  mat, vec, mat, vec, vec, vec, *cast_specs],
        out_specs=[pl.BlockSpec((ts, c), lambda b, i: (row(b, i), 0)), *cast_specs],
        out_shape=[jax.ShapeDtypeStruct((bsz * s, c), BF16), *cast_shapes],
        scratch_shapes=[pltpu.VMEM((ts + SUBLANE, c), F32),
                        pltpu.VMEM((SUBLANE, c), F32),
                        pltpu.VMEM((ts, c), F32),
                        pltpu.VMEM((ts, c), F32)],
        compiler_params=_cparams(("arbitrary", "arbitrary")),
        name="rg_lru",
    )(proj, proj, cw, cb, wa, ba, wx, bx, lam, g, *cast)


def _compress_one(kv_ref, pe_ref, w1_ref, b1_ref, w2_ref):
    s, dh = kv_ref.shape
    n = s // CMP_STRIDE
    half = CMP_STRIDE * dh
    xa = jnp.concatenate([kv_ref[pl.ds(l, n, stride=CMP_STRIDE), :] for l in range(CMP_STRIDE)],
                         axis=1).astype(BF16)
    lo = jnp.dot(xa, w1_ref[0:half, :], preferred_element_type=F32)
    hi = jnp.dot(xa, w1_ref[half:2 * half, :], preferred_element_type=F32)
    pe = jnp.broadcast_to(pe_ref[...], (SUBLANE, 2 * half)).astype(BF16)
    pe_term = jnp.dot(pe, w1_ref[...], preferred_element_type=F32)[0:1, :]
    pre = lo + pltpu.roll(hi, n - 1, 0) + pe_term + b1_ref[...]
    return jnp.dot(jax.nn.gelu(pre).astype(BF16), w2_ref[...], preferred_element_type=F32)


def _compress_kernel(kc_ref, vc_ref, pek_ref, w1k_ref, b1k_ref, w2k_ref,
                     pev_ref, w1v_ref, b1v_ref, w2v_ref, ko_ref, vo_ref):
    ko_ref[0] = _compress_one(kc_ref, pek_ref, w1k_ref, b1k_ref, w2k_ref)
    vo_ref[0] = _compress_one(vc_ref, pev_ref, w1v_ref, b1v_ref, w2v_ref)


def _compress(kvc, bsz, s, pk, pv):
    dh = HEAD_DIM
    kc_blk, vc_blk = 0, N_KV_HEADS
    n = s // CMP_STRIDE
    full = lambda a: pl.BlockSpec(a.shape, lambda b, h: (0,) * a.ndim)
    out = pl.BlockSpec((1, n, dh), lambda b, h: (b * N_KV_HEADS + h, 0, 0))
    shp = jax.ShapeDtypeStruct((bsz * N_KV_HEADS, n, dh), F32)
    return pl.pallas_call(
        _compress_kernel,
        grid=(bsz, N_KV_HEADS),
        in_specs=[pl.BlockSpec((s, dh), lambda b, h: (b, kc_blk + h)),
                  pl.BlockSpec((s, dh), lambda b, h: (b, vc_blk + h)),
                  *[full(a) for a in pk], *[full(a) for a in pv]],
        out_specs=[out, out],
        out_shape=[shp, shp],
        compiler_params=_cparams(("arbitrary", "arbitrary")),
        name="compress_kv",
    )(kvc, kvc, *pk, *pv)


KEY_CHUNK = 512
ROW_GROUPS = 2


def _attn_kernel(q_ref, ksel_ref, vsel_ref, kw_ref, vw_ref, gate_ref, kcmp_ref, vcmp_ref, ovt_ref, *rest, n_cast):
    cast_in, o_ref, cast_out = rest[:n_cast], rest[n_cast], rest[n_cast + 1:2 * n_cast + 1]
    (kaug_ref, vsaug_ref, kwb_ref, vwaug_ref, dbias_ref, wbias_ref,
     qaug_ref, s_a, s_b, p_a, p_b, acc_ref, m_ref, a_ref) = rest[2 * n_cast + 1:]
    for src, dst in zip(cast_in, cast_out):
        dst[...] = src[...].astype(dst.dtype)
    i = pl.program_id(2)
    s_len, dh = ksel_ref.shape
    tq = q_ref.shape[0]
    n_cmp = kcmp_ref.shape[1]
    n_sel = s_len // SEL_BLOCK
    gh = GQA // ROW_GROUPS
    rg = gh * tq

    @pl.when(i == 0)
    def _():
        kaug_ref[:, 0:dh] = ksel_ref[...].astype(BF16)
        kblk = lax.broadcasted_iota(jnp.int32, (s_len, LANE), 0) // SEL_BLOCK
        lane = lax.broadcasted_iota(jnp.int32, (s_len, LANE), 1)
        kaug_ref[:, dh:dh + LANE] = jnp.where(kblk == lane, 1.0, 0.0).astype(BF16)
        ones = jnp.ones((s_len, LANE), BF16)
        vsaug_ref[:, 0:dh] = vsel_ref[...].astype(BF16)
        vsaug_ref[:, dh:dh + LANE] = ones
        kwb_ref[...] = kw_ref[...].astype(BF16)
        vwaug_ref[:, 0:dh] = vw_ref[...].astype(BF16)
        vwaug_ref[:, dh:dh + LANE] = ones
        for v in range(dbias_ref.shape[0]):
            col = lax.broadcasted_iota(jnp.int32, (rg, KEY_CHUNK), 1)
            tok = lax.broadcasted_iota(jnp.int32, (rg, KEY_CHUNK), 0) & (tq - 1)
            dbias_ref[v] = jnp.where(col <= v * tq + tok, 0.0, NEG_INF)
        for v in range(wbias_ref.shape[0]):
            col = lax.broadcasted_iota(jnp.int32, wbias_ref.shape[1:], 1)
            tok = lax.broadcasted_iota(jnp.int32, wbias_ref.shape[1:], 0) & (tq - 1)
            d = v * tq + tok - col
            wbias_ref[v] = jnp.where(d >= 0, jnp.where(d < WINDOW, 0.0, NEG_INF), NEG_INF)

    nt = (((1,), (1,)), ((), ()))
    qs = i * tq
    qf = q_ref[...]
    qb = [jnp.concatenate([qf[:, g * dh:(g + 1) * dh] for g in range(sp * gh, (sp + 1) * gh)],
                          axis=0).astype(BF16) for sp in range(ROW_GROUPS)]
    t_row = qs + (lax.broadcasted_iota(jnp.int32, (rg, 1), 0) & (tq - 1))

    wlen = wbias_ref.shape[2]
    w0 = pl.multiple_of(jnp.maximum(qs + tq - wlen, 0), tq)
    band_bias = wbias_ref[jnp.minimum(i, wbias_ref.shape[0] - 1)]
    kcb = kcmp_ref[0].astype(BF16)
    vcb = vcmp_ref[0].astype(BF16)
    vis_c = (lax.broadcasted_iota(jnp.int32, (rg, n_cmp), 1) * CMP_STRIDE + (CMP_BLOCK - 1)) <= t_row

    def cmp_scores(sp):
        return lax.dot_general(qb[sp], kcb, nt, preferred_element_type=F32)

    def cmp_probs(s_c):
        s_c = jnp.where(vis_c, s_c, NEG_INF)
        e_c = jnp.where(vis_c, jnp.exp(s_c - jnp.max(s_c, axis=1, keepdims=True)), 0.0)
        den_c = jnp.sum(e_c, axis=1, keepdims=True)
        return e_c / jnp.where(den_c > 0.0, den_c, 1.0)

    def win_scores(sp):
        return lax.dot_general(qb[sp], kwb_ref[pl.ds(w0, wlen), :], nt, preferred_element_type=F32)

    def win_probs(s_w):
        s_w = s_w + band_bias
        return jnp.exp(s_w - jnp.max(s_w, axis=1, keepdims=True)).astype(BF16)

    def win_out(e_w):
        pv = jnp.dot(e_w, vwaug_ref[pl.ds(w0, wlen), :], preferred_element_type=F32)
        return pv[:, :dh] / pv[:, dh:]

    s_c = [cmp_scores(sp) for sp in range(ROW_GROUPS)]
    s_w = [win_scores(sp) for sp in range(ROW_GROUPS)]
    p_c, o_c, o_w, e_w = [], [], [], []
    for sp in range(ROW_GROUPS):
        p_c.append(cmp_probs(s_c[sp]))
        o_c.append(jnp.dot(p_c[sp].astype(BF16), vcb, preferred_element_type=F32))
    p_sum = None
    for sp in range(ROW_GROUPS):
        for g in range(gh):
            pg = p_c[sp][g * tq:(g + 1) * tq]
            p_sum = pg if p_sum is None else p_sum + pg

    imp_t = lax.dot_general(ovt_ref[...], p_sum, nt, precision=lax.Precision.HIGHEST,
                            preferred_element_type=F32)
    blk = lax.broadcasted_iota(jnp.int32, (n_sel, tq), 0)
    cur = (qs + lax.broadcasted_iota(jnp.int32, (n_sel, tq), 1)) // SEL_BLOCK
    age = cur - blk
    score = jnp.where(blk == 0, FORCE_SCORE,
                      jnp.where(age < 0, -1.0, jnp.where(age < N_LOCAL_SEL, FORCE_SCORE, imp_t)))
    parts = [jnp.zeros((n_sel, tq), jnp.int32) for _ in range(4)]
    for j in range(n_sel):
        rj = score[j:j + 1, :]
        parts[j % 4] = parts[j % 4] + jnp.where(blk > j, jnp.where(rj >= score, 1, 0),
                                                jnp.where(rj > score, 1, 0))
    rank = (parts[0] + parts[1]) + (parts[2] + parts[3])
    bias_t = jnp.where(rank < min(N_SEL, n_sel), 0.0, NEG_INF)
    bias_t = jnp.concatenate([bias_t, jnp.zeros((LANE - n_sel, tq), F32)], axis=0)
    bias = jnp.concatenate([bias_t.T.astype(BF16)] * gh, axis=0)
    for sp in range(ROW_GROUPS):
        qaug_ref[sp] = jnp.concatenate([qb[sp], bias], axis=1)

    def scores(sp, k0):
        return lax.dot_general(qaug_ref[sp], kaug_ref[pl.ds(k0, KEY_CHUNK), :], nt,
                               preferred_element_type=F32)

    def accumulate(sp, p, alpha, k0):
        pv = jnp.dot(p, vsaug_ref[pl.ds(k0, KEY_CHUNK), :], preferred_element_type=F32)
        acc_ref[sp] = alpha * acc_ref[sp] + pv

    def soften(sp, s):
        m_old = m_ref[sp]
        m_new = jnp.maximum(m_old, jnp.max(s, axis=1, keepdims=True))
        m_ref[sp] = m_new
        return jnp.exp(m_old - m_new), jnp.exp(s - m_new).astype(BF16)

    n_full = qs // KEY_CHUNK
    k_diag = pl.multiple_of(n_full * KEY_CHUNK, KEY_CHUNK)

    def step(c, s_in, p_prev, s_out, p_out):
        k_next = pl.multiple_of((c + 1) * KEY_CHUNK, KEY_CHUNK)
        k_prev = pl.multiple_of(jnp.where(c == 0, n_full, c - 1) * KEY_CHUNK, KEY_CHUNK)
        for sp in range(ROW_GROUPS):
            accumulate(sp, p_prev[sp], a_ref[sp], k_prev)
            if s_out is not None:
                s_out[sp] = scores(sp, k_next)
        for sp in range(ROW_GROUPS):
            a_ref[sp], p_out[sp] = soften(sp, s_in[sp])

    causal_bias = dbias_ref[i % (KEY_CHUNK // tq)]
    for sp in range(ROW_GROUPS):
        m_ref[sp] = jnp.full((rg, 1), NEG_INF, F32)
        acc_ref[sp] = jnp.zeros((rg, dh + LANE), F32)
        a_ref[sp], p_b[sp] = soften(sp, scores(sp, k_diag) + causal_bias)
        s_a[sp] = scores(sp, 0)

    for sp in range(ROW_GROUPS):
        e_w.append(win_probs(s_w[sp]))
        o_w.append(win_out(e_w[sp]))
    gate = jax.nn.sigmoid(gate_ref[...])
    for g in range(GQA):
        sp, gg = divmod(g, gh)
        sl = slice(gg * tq, (gg + 1) * tq)
        o_ref[:, g * dh:(g + 1) * dh] = (gate[:, N_BRANCH * g:N_BRANCH * g + 1] * o_c[sp][sl]
                                         + gate[:, N_BRANCH * g + 2:N_BRANCH * g + 3] * o_w[sp][sl])

    def pair(k, carry):
        step(2 * k, s_a, p_b, s_b, p_a)
        step(2 * k + 1, s_b, p_a, s_a, p_b)
        return carry

    n_steps = jnp.maximum(n_full - 1, 0)
    lax.fori_loop(0, n_steps // 2, pair, 0)
    k_last = pl.multiple_of(n_steps * KEY_CHUNK, KEY_CHUNK)

    @pl.when(n_full == 0)
    def _():
        for sp in range(ROW_GROUPS):
            accumulate(sp, p_b[sp], a_ref[sp], k_diag)

    @pl.when((n_full > 0) & (n_steps % 2 == 0))
    def _():
        step(n_steps, s_a, p_b, None, p_a)
        for sp in range(ROW_GROUPS):
            accumulate(sp, p_a[sp], a_ref[sp], k_last)

    @pl.when((n_full > 0) & (n_steps % 2 == 1))
    def _():
        step(n_steps - 1, s_a, p_b, s_b, p_a)
        step(n_steps, s_b, p_a, None, p_b)
        for sp in range(ROW_GROUPS):
            accumulate(sp, p_b[sp], a_ref[sp], k_last)

    gate = jax.nn.sigmoid(gate_ref[...])
    for g in range(GQA):
        sp, gg = divmod(g, gh)
        acc = acc_ref[sp, gg * tq:(gg + 1) * tq, :]
        o_ref[:, g * dh:(g + 1) * dh] += gate[:, N_BRANCH * g + 1:N_BRANCH * g + 2] * (acc[:, :dh] / acc[:, dh:])


def _attention(q, kv4, gates, k_cmp, v_cmp, ov_t, bsz, s, cast=()):
    dh = HEAD_DIM
    tq = Q_BLOCK
    nq = s // tq
    rg = (GQA // ROW_GROUPS) * tq
    n_cmp = k_cmp.shape[1]
    kv = lambda blk: pl.BlockSpec((s, dh), lambda b, h, i: (b, blk + h))
    cmp_spec = pl.BlockSpec((1, n_cmp, dh), lambda b, h, i: (b * N_KV_HEADS + h, 0, 0))
    cast_specs, cast_shapes = _cast_rider_specs(cast, bsz * N_KV_HEADS * nq,
                                                lambda b, h, i: ((b * N_KV_HEADS + h) * nq + i, 0))
    return pl.pallas_call(
        functools.partial(_attn_kernel, n_cast=len(cast)),
        grid=(bsz, N_KV_HEADS, nq),
        in_specs=[pl.BlockSpec((tq, GQA * dh), lambda b, h, i: (b * nq + i, h)),
                  kv(0), kv(N_KV_HEADS), kv(2 * N_KV_HEADS), kv(3 * N_KV_HEADS),
                  pl.BlockSpec((tq, LANE), lambda b, h, i: (b * nq + i, h)),
                  cmp_spec, cmp_spec,
                  pl.BlockSpec(ov_t.shape, lambda b, h, i: (0, 0)), *cast_specs],
        out_specs=[pl.BlockSpec((tq, GQA * dh), lambda b, h, i: (b * nq + i, h)), *cast_specs],
        out_shape=[jax.ShapeDtypeStruct((bsz * s, N_HEADS * dh), F32), *cast_shapes],
        scratch_shapes=[pltpu.VMEM((s, dh + LANE), BF16),
                        pltpu.VMEM((s, dh + LANE), BF16),
                        pltpu.VMEM((s, dh), BF16),
                        pltpu.VMEM((s, dh + LANE), BF16),
                        pltpu.VMEM((KEY_CHUNK // tq, rg, KEY_CHUNK), F32),
                        pltpu.VMEM((WINDOW // tq + 1, rg, min(WINDOW + tq, s)), F32),
                        pltpu.VMEM((ROW_GROUPS, rg, dh + LANE), BF16),
                        pltpu.VMEM((ROW_GROUPS, rg, KEY_CHUNK), F32),
                        pltpu.VMEM((ROW_GROUPS, rg, KEY_CHUNK), F32),
                        pltpu.VMEM((ROW_GROUPS, rg, KEY_CHUNK), BF16),
                        pltpu.VMEM((ROW_GROUPS, rg, KEY_CHUNK), BF16),
                        pltpu.VMEM((ROW_GROUPS, rg, dh + LANE), F32),
                        pltpu.VMEM((ROW_GROUPS, rg, 1), F32),
                        pltpu.VMEM((ROW_GROUPS, rg, 1), F32)],
        compiler_params=_cparams(("arbitrary", "arbitrary", "arbitrary")),
        name="nsa_attention",
    )(q, kv4, kv4, kv4, kv4, gates, k_cmp, v_cmp, ov_t, *cast)


def _outproj_kernel(lru_ref, attn_ref, ga_ref, x_ref, w_ref, gf_ref, *rest, n_cast):
    cast_in, (h_ref, xn_ref), cast_out = rest[:n_cast], rest[n_cast:n_cast + 2], rest[n_cast + 2:]
    for src, dst in zip(cast_in, cast_out):
        dst[...] = src[...].astype(dst.dtype)
    c = lru_ref.shape[1]
    an = _rms(attn_ref[...], ga_ref[...]).astype(BF16)
    h = (x_ref[...]
         + jnp.dot(lru_ref[...], w_ref[0:c, :], preferred_element_type=F32)
         + jnp.dot(an, w_ref[c:, :], preferred_element_type=F32))
    h_ref[...] = h
    xn_ref[...] = _rms(h, gf_ref[...]).astype(BF16)


def _out_proj(lru_n, attn, g_attn, x2, w_out, g_ffn, tm, cast=()):
    t, d = x2.shape
    c = lru_n.shape[1]
    cast_specs, cast_shapes = _cast_rider_specs(cast, t // tm, lambda i: (i, 0))
    return pl.pallas_call(
        functools.partial(_outproj_kernel, n_cast=len(cast)),
        grid=(t // tm,),
        in_specs=[pl.BlockSpec((tm, c), lambda i: (i, 0)),
                  pl.BlockSpec((tm, attn.shape[1]), lambda i: (i, 0)),
                  pl.BlockSpec((1, attn.shape[1]), lambda i: (0, 0)),
                  pl.BlockSpec((tm, d), lambda i: (i, 0)),
                  pl.BlockSpec(w_out.shape, lambda i: (0, 0), pipeline_mode=pl.Buffered(1)),
                  pl.BlockSpec((1, d), lambda i: (0, 0)), *cast_specs],
        out_specs=[pl.BlockSpec((tm, d), lambda i: (i, 0)),
                   pl.BlockSpec((tm, d), lambda i: (i, 0)), *cast_specs],
        out_shape=[jax.ShapeDtypeStruct((t, d), F32), jax.ShapeDtypeStruct((t, d), BF16), *cast_shapes],
        compiler_params=_cparams(("arbitrary",)),
        name="out_proj",
    )(lru_n, attn, g_attn, x2, w_out, g_ffn, *cast)


def _ffn_kernel(xn_ref, wu_ref, wv_ref, cw_ref, cb_ref, wd_ref, h_ref, g_ref, o_ref,
                halo_ref, upad_ref, *, tiles_per_seq, final_norm):
    i = pl.program_id(0)
    j = pl.program_id(1)
    tm = xn_ref.shape[0]

    @pl.when(j == 0)
    def _():
        o_ref[...] = h_ref[...]

    xn = xn_ref[...]
    u = jnp.dot(xn, wu_ref[...], preferred_element_type=F32)
    v = jnp.dot(xn, wv_ref[...], preferred_element_type=F32)
    first = (i % tiles_per_seq) == 0
    upad_ref[0:SUBLANE, :] = jnp.where(first, 0.0, halo_ref[j])
    upad_ref[SUBLANE:, :] = u
    halo_ref[j] = u[tm - SUBLANE:, :]
    uc = cb_ref[...] + cw_ref[FFN_CONV - 1:FFN_CONV, :] * u
    for k in range(FFN_CONV - 1):
        off = SUBLANE - (FFN_CONV - 1) + k
        uc = uc + cw_ref[k:k + 1, :] * upad_ref[off:off + tm, :]
    act = (jax.nn.gelu(uc) * v).astype(BF16)
    o_ref[...] += jnp.dot(act, wd_ref[...], preferred_element_type=F32)

    if final_norm:
        @pl.when(j == pl.num_programs(1) - 1)
        def _():
            o_ref[...] = _rms(o_ref[...], g_ref[...])


def _ffn(xn, h1, w_up, cw, cb, w_down, g_final, final_norm, s, tm, tf):
    t, d = xn.shape
    dff = w_down.shape[0]
    nj = dff // tf
    return pl.pallas_call(
        functools.partial(_ffn_kernel, tiles_per_seq=s // tm, final_norm=final_norm),
        grid=(t // tm, nj),
        in_specs=[pl.BlockSpec((tm, d), lambda i, j: (i, 0)),
                  pl.BlockSpec((d, tf), lambda i, j: (0, j)),
                  pl.BlockSpec((d, tf), lambda i, j: (0, nj + j)),
                  pl.BlockSpec((FFN_CONV, tf), lambda i, j: (0, j)),
                  pl.BlockSpec((1, tf), lambda i, j: (0, j)),
                  pl.BlockSpec((tf, d), lambda i, j: (j, 0)),
                  pl.BlockSpec((tm, d), lambda i, j: (i, 0)),
                  pl.BlockSpec((1, d), lambda i, j: (0, 0))],
        out_specs=pl.BlockSpec((tm, d), lambda i, j: (i, 0)),
        out_shape=jax.ShapeDtypeStruct((t, d), F32),
        scratch_shapes=[pltpu.VMEM((nj, SUBLANE, tf), F32),
                        pltpu.VMEM((tm + SUBLANE, tf), F32)],
        compiler_params=_cparams(("arbitrary", "arbitrary")),
        name="conv_ffn",
    )(xn, w_up, w_up, cw, cb, w_down, h1, g_final)


def _overlap_t(n_cmp, n_sel):
    c0 = jnp.arange(n_cmp)[None, :] * CMP_STRIDE
    s0 = jnp.arange(n_sel)[:, None] * SEL_BLOCK
    ov = jnp.clip(jnp.minimum(c0 + CMP_BLOCK, s0 + SEL_BLOCK) - jnp.maximum(c0, s0), 0)
    return ov.astype(F32) / CMP_BLOCK


def kernel(x, g_mix, w_in, lru_conv_w, lru_conv_b, lru_wa, lru_ba, lru_wx, lru_bx, lru_lambda, cmp_pe_k, cmp_w1_k, cmp_b1_k, cmp_w2_k, cmp_pe_v, cmp_w1_v, cmp_b1_v, cmp_w2_v, g_lru_out, g_attn_out, w_out, g_ffn, w_up, ffn_conv_w, ffn_conv_b, w_down, g_final):
    bsz, s, d = x.shape
    depth = w_in.shape[0]
    c_lru = lru_conv_w.shape[2]
    attn_w = N_HEADS * HEAD_DIM
    kv_w = N_KV_HEADS * HEAD_DIM
    n_gate = N_BRANCH * GQA
    main_w = 2 * c_lru + attn_w + 6 * kv_w
    segments = ((0, 2 * c_lru, F32, None),
                (2 * c_lru, attn_w, BF16, HEAD_DIM ** -0.5),
                (2 * c_lru + attn_w, 2 * kv_w, F32, None),
                (2 * c_lru + attn_w + 2 * kv_w, 4 * kv_w, BF16, None))
    ov_t = _overlap_t(s // CMP_STRIDE, s // SEL_BLOCK)
    row = lambda a: a.reshape(1, -1)

    h = x.reshape(bsz * s, d)
    for l in range(depth):
        w_main = w_in[l].astype(BF16)
        gpad = jnp.zeros((d, LANE - n_gate), F32)
        w_gate = jnp.concatenate([p for hh in range(N_KV_HEADS)
                                  for p in (w_in[l][:, main_w + hh * n_gate:main_w + (hh + 1) * n_gate], gpad)],
                                 axis=1).astype(BF16)
        lru_in, q, kvc, kv4, gates = _in_proj(h, row(g_mix[l]), w_main, segments, w_gate, tm=512)
        (lru_n,) = _lru(lru_in, bsz, s, lru_conv_w[l], row(lru_conv_b[l]), lru_wa[l].astype(BF16),
                        row(lru_ba[l]), lru_wx[l].astype(BF16), row(lru_bx[l]), row(lru_lambda[l]),
                        row(g_lru_out[l]), ts=256)
        pk = (cmp_pe_k[l].reshape(1, -1), cmp_w1_k[l].astype(BF16), row(cmp_b1_k[l]), cmp_w2_k[l].astype(BF16))
        pv = (cmp_pe_v[l].reshape(1, -1), cmp_w1_v[l].astype(BF16), row(cmp_b1_v[l]), cmp_w2_v[l].astype(BF16))
        k_cmp, v_cmp = _compress(kvc, bsz, s, pk, pv)
        attn, w_out_b, w_down_b, w_up_b = _attention(q, kv4, gates, k_cmp, v_cmp, ov_t, bsz, s,
                                                     cast=(w_out[l], w_down[l], w_up[l]))
        h1, xn = _out_proj(lru_n, attn, row(g_attn_out[l]), h, w_out_b, row(g_ffn[l]), tm=512)
        h = _ffn(xn, h1, w_up_b, ffn_conv_w[l], row(ffn_conv_b[l]), w_down_b,
                 row(g_final), l == depth - 1, s, tm=512, tf=1024)
    return h.reshape(bsz, s, d)
```

```python
import functools

import jax
import jax.numpy as jnp
from jax import lax
from jax.experimental import pallas as pl
from jax.experimental.pallas import tpu as pltpu

F32 = jnp.float32
BF16 = jnp.bfloat16

LANE = 128
SUBLANE = 8
VMEM_LIMIT = 56 * 1024 * 1024

LRU_HEADS = 8
LRU_CONV = 4
LRU_C = 8.0
N_HEADS = 8
N_KV_HEADS = 2
GQA = N_HEADS // N_KV_HEADS
HEAD_DIM = 128
N_BRANCH = 3
CMP_BLOCK = 32
CMP_STRIDE = 16
SEL_BLOCK = 64
N_SEL = 16
N_LOCAL_SEL = 2
WINDOW = 512
Q_BLOCK = 256
FFN_CONV = 3
EPS = 1e-6
NEG_INF = -1e30
FORCE_SCORE = 1e4


def _rms(x, g):
    return x * lax.rsqrt(jnp.mean(x * x, axis=-1, keepdims=True) + EPS) * g


def _cparams(sem):
    return pltpu.CompilerParams(dimension_semantics=sem, vmem_limit_bytes=VMEM_LIMIT)


def _inproj_kernel(x_ref, g_ref, w_ref, wg_ref, *o_refs, segments):
    xn = _rms(x_ref[...], g_ref[...]).astype(BF16)
    for (c0, width, scale), o_ref in zip(segments, o_refs):
        y = jnp.dot(xn, w_ref[:, c0:c0 + width], preferred_element_type=F32)
        o_ref[...] = (y if scale is None else y * scale).astype(o_ref.dtype)
    o_refs[-1][...] = jnp.dot(xn, wg_ref[...], preferred_element_type=F32)


def _in_proj(x2, g, w_all, segments, w_gate, tm):
    t, d = x2.shape
    ng = w_gate.shape[1]
    n = max(c0 + width for c0, width, _, _ in segments)
    resident = lambda shape: pl.BlockSpec(shape, lambda i: (0, 0), pipeline_mode=pl.Buffered(1))
    return pl.pallas_call(
        functools.partial(_inproj_kernel, segments=tuple((c0, width, sc) for c0, width, _, sc in segments)),
        grid=(t // tm,),
        in_specs=[pl.BlockSpec((tm, d), lambda i: (i, 0)),
                  pl.BlockSpec((1, d), lambda i: (0, 0)),
                  resident((d, n)), resident(w_gate.shape)],
        out_specs=[*[pl.BlockSpec((tm, width), lambda i: (i, 0)) for _, width, _, _ in segments],
                   pl.BlockSpec((tm, ng), lambda i: (i, 0))],
        out_shape=[*[jax.ShapeDtypeStruct((t, width), dt) for _, width, dt, _ in segments],
                   jax.ShapeDtypeStruct((t, ng), F32)],
        compiler_params=_cparams(("arbitrary",)),
        name="in_proj",
    )(x2, g, w_all, w_gate)


def _cast_rider_specs(arrays, n_steps, index):
    specs, shapes = [], []
    for a in arrays:
        rows = a.shape[0] // n_steps
        assert rows * n_steps == a.shape[0] and rows % (2 * SUBLANE) == 0
        specs.append(pl.BlockSpec((rows, a.shape[1]), index))
        shapes.append(jax.ShapeDtypeStruct(a.shape, BF16))
    return specs, shapes


def _lru_kernel(x_ref, gate_ref, cw_ref, cb_ref, wa_ref, ba_ref, wx_ref, bx_ref, lam_ref, g_ref, *rest, n_cast):
    cast_in, o_ref, cast_out = rest[:n_cast], rest[n_cast], rest[n_cast + 1:2 * n_cast + 1]
    xpad_ref, carry_ref, a_ref, b_ref = rest[2 * n_cast + 1:]
    for src, dst in zip(cast_in, cast_out):
        dst[...] = src[...].astype(dst.dtype)
    ts, c = x_ref.shape
    hb = c // LRU_HEADS

    @pl.when(pl.program_id(1) == 0)
    def _():
        xpad_ref[...] = jnp.zeros_like(xpad_ref)
        carry_ref[...] = jnp.zeros_like(carry_ref)

    xpad_ref[0:SUBLANE, :] = xpad_ref[ts:ts + SUBLANE, :]
    xpad_ref[SUBLANE:, :] = x_ref[...]
    u = cb_ref[...] + cw_ref[LRU_CONV - 1:LRU_CONV, :] * x_ref[...]
    for k in range(LRU_CONV - 1):
        off = SUBLANE - (LRU_CONV - 1) + k
        u = u + cw_ref[k:k + 1, :] * xpad_ref[off:off + ts, :]

    ub = u.astype(BF16)
    r_parts, i_parts = [], []
    for h in range(LRU_HEADS):
        uh = ub[:, h * hb:(h + 1) * hb]
        r_parts.append(jnp.dot(uh, wa_ref[h], preferred_element_type=F32))
        i_parts.append(jnp.dot(uh, wx_ref[h], preferred_element_type=F32))
    r = jax.nn.sigmoid(jnp.concatenate(r_parts, axis=1) + ba_ref[...])
    ig = jax.nn.sigmoid(jnp.concatenate(i_parts, axis=1) + bx_ref[...])
    nl = -lam_ref[...]
    softplus = jnp.maximum(nl, 0.0) + jnp.log1p(jnp.exp(-jnp.abs(nl)))
    log_a = (-LRU_C) * r * softplus
    a = jnp.exp(log_a)
    a_ref[...] = a
    z = -jnp.tanh(log_a) * (a * a + 1.0)
    root = jnp.where(z > 0.0, z * lax.rsqrt(z), 0.0)
    b_ref[...] = root * (ig * u)

    row = lax.broadcasted_iota(jnp.int32, (SUBLANE, c), 0)

    def group(gi, carry):
        r0 = pl.multiple_of(gi * SUBLANE, SUBLANE)
        a8 = a_ref[pl.ds(r0, SUBLANE), :]
        b8 = b_ref[pl.ds(r0, SUBLANE), :]
        for d in (1, 2, 4):
            a_sh = pltpu.roll(a8, d, 0)
            b_sh = pltpu.roll(b8, d, 0)
            m = row >= d
            b8 = jnp.where(m, a8 * b_sh + b8, b8)
            a8 = jnp.where(m, a8 * a_sh, a8)
        h8 = a8 * carry + b8
        b_ref[pl.ds(r0, SUBLANE), :] = h8
        return jnp.broadcast_to(h8[SUBLANE - 1:SUBLANE, :], (SUBLANE, c))

    carry_ref[...] = lax.fori_loop(0, ts // SUBLANE, group, carry_ref[...])

    y = b_ref[...] * jax.nn.gelu(gate_ref[...])
    o_ref[...] = _rms(y, g_ref[...]).astype(o_ref.dtype)


def _lru(proj, bsz, s, cw, cb, wa, ba, wx, bx, lam, g, ts, cast=()):
    c = cw.shape[1]
    ns = s // ts
    row = lambda b, i: b * ns + i
    vec = pl.BlockSpec((1, c), lambda b, i: (0, 0))
    mat = pl.BlockSpec(wa.shape, lambda b, i: (0, 0, 0))
    cast_specs, cast_shapes = _cast_rider_specs(cast, bsz * ns, lambda b, i: (row(b, i), 0))
    return pl.pallas_call(
        functools.partial(_lru_kernel, n_cast=len(cast)),
        grid=(bsz, ns),
        in_specs=[pl.BlockSpec((ts, c), lambda b, i: (row(b, i), 0)),
                  pl.BlockSpec((ts, c), lambda b, i: (row(b, i), 1)),
                  pl.BlockSpec((LRU_CONV, c), lambda b, i: (0, 0)), vec,
                  mat, vec, mat, vec, vec, vec, *cast_specs],
        out_specs=[pl.BlockSpec((ts, c), lambda b, i: (row(b, i), 0)), *cast_specs],
        out_shape=[jax.ShapeDtypeStruct((bsz * s, c), BF16), *cast_shapes],
        scratch_shapes=[pltpu.VMEM((ts + SUBLANE, c), F32),
                        pltpu.VMEM((SUBLANE, c), F32),
                        pltpu.VMEM((ts, c), F32),
                        pltpu.VMEM((ts, c), F32)],
        compiler_params=_cparams(("arbitrary", "arbitrary")),
        name="rg_lru",
    )(proj, proj, cw, cb, wa, ba, wx, bx, lam, g, *cast)


def _compress_one(kv_ref, pe_ref, w1_ref, b1_ref, w2_ref):
    s, dh = kv_ref.shape
    n = s // CMP_STRIDE
    half = CMP_STRIDE * dh
    xa = jnp.concatenate([kv_ref[pl.ds(l, n, stride=CMP_STRIDE), :] for l in range(CMP_STRIDE)],
                         axis=1).astype(BF16)
    lo = jnp.dot(xa, w1_ref[0:half, :], preferred_element_type=F32)
    hi = jnp.dot(xa, w1_ref[half:2 * half, :], preferred_element_type=F32)
    pe = jnp.broadcast_to(pe_ref[...], (SUBLANE, 2 * half)).astype(BF16)
    pe_term = jnp.dot(pe, w1_ref[...], preferred_element_type=F32)[0:1, :]
    pre = lo + pltpu.roll(hi, n - 1, 0) + pe_term + b1_ref[...]
    return jnp.dot(jax.nn.gelu(pre).astype(BF16), w2_ref[...], preferred_element_type=F32)


def _compress_kernel(kc_ref, vc_ref, pek_ref, w1k_ref, b1k_ref, w2k_ref,
                     pev_ref, w1v_ref, b1v_ref, w2v_ref, ko_ref, vo_ref):
    ko_ref[0] = _compress_one(kc_ref, pek_ref, w1k_ref, b1k_ref, w2k_ref)
    vo_ref[0] = _compress_one(vc_ref, pev_ref, w1v_ref, b1v_ref, w2v_ref)


def _compress(kvc, bsz, s, pk, pv):
    dh = HEAD_DIM
    kc_blk, vc_blk = 0, N_KV_HEADS
    n = s // CMP_STRIDE
    full = lambda a: pl.BlockSpec(a.shape, lambda b, h: (0,) * a.ndim)
    out = pl.BlockSpec((1, n, dh), lambda b, h: (b * N_KV_HEADS + h, 0, 0))
    shp = jax.ShapeDtypeStruct((bsz * N_KV_HEADS, n, dh), F32)
    return pl.pallas_call(
        _compress_kernel,
        grid=(bsz, N_KV_HEADS),
        in_specs=[pl.BlockSpec((s, dh), lambda b, h: (b, kc_blk + h)),
                  pl.BlockSpec((s, dh), lambda b, h: (b, vc_blk + h)),
                  *[full(a) for a in pk], *[full(a) for a in pv]],
        out_specs=[out, out],
        out_shape=[shp, shp],
        compiler_params=_cparams(("arbitrary", "arbitrary")),
        name="compress_kv",
    )(kvc, kvc, *pk, *pv)


KEY_CHUNK = 512
ROW_GROUPS = 2


def _attn_kernel(q_ref, ksel_ref, vsel_ref, kw_ref, vw_ref, gate_ref, kcmp_ref, vcmp_ref, ovt_ref, *rest, n_cast):
    cast_in, o_ref, cast_out = rest[:n_cast], rest[n_cast], rest[n_cast + 1:2 * n_cast + 1]
    (kaug_ref, vsaug_ref, kwb_ref, vwaug_ref, dbias_ref, wbias_ref,
     qaug_ref, s_a, s_b, p_a, p_b, acc_ref, m_ref, a_ref,
     qb_ref, sc_ref, pc_ref, oc_ref, sw_ref, ew_ref, ow_ref) = rest[2 * n_cast + 1:]
    for src, dst in zip(cast_in, cast_out):
        dst[...] = src[...].astype(dst.dtype)
    i = pl.program_id(2)
    s_len, dh = ksel_ref.shape
    tq = q_ref.shape[0]
    n_cmp = kcmp_ref.shape[1]
    n_sel = s_len // SEL_BLOCK
    gh = GQA // ROW_GROUPS
    rg = gh * tq

    @pl.when(i == 0)
    def _():
        kaug_ref[:, 0:dh] = ksel_ref[...].astype(BF16)
        kblk = lax.broadcasted_iota(jnp.int32, (s_len, LANE), 0) // SEL_BLOCK
        lane = lax.broadcasted_iota(jnp.int32, (s_len, LANE), 1)
        kaug_ref[:, dh:dh + LANE] = jnp.where(kblk == lane, 1.0, 0.0).astype(BF16)
        ones = jnp.ones((s_len, LANE), BF16)
        vsaug_ref[:, 0:dh] = vsel_ref[...].astype(BF16)
        vsaug_ref[:, dh:dh + LANE] = ones
        kwb_ref[...] = kw_ref[...].astype(BF16)
        vwaug_ref[:, 0:dh] = vw_ref[...].astype(BF16)
        vwaug_ref[:, dh:dh + LANE] = ones
        for v in range(dbias_ref.shape[0]):
            col = lax.broadcasted_iota(jnp.int32, (rg, KEY_CHUNK), 1)
            tok = lax.broadcasted_iota(jnp.int32, (rg, KEY_CHUNK), 0) & (tq - 1)
            dbias_ref[v] = jnp.where(col <= v * tq + tok, 0.0, NEG_INF)
        for v in range(wbias_ref.shape[0]):
            col = lax.broadcasted_iota(jnp.int32, wbias_ref.shape[1:], 1)
            tok = lax.broadcasted_iota(jnp.int32, wbias_ref.shape[1:], 0) & (tq - 1)
            d = v * tq + tok - col
            wbias_ref[v] = jnp.where(d >= 0, jnp.where(d < WINDOW, 0.0, NEG_INF), NEG_INF)

    nt = (((1,), (1,)), ((), ()))
    qs = i * tq
    qf = q_ref[...]
    qb = [jnp.concatenate([qf[:, g * dh:(g + 1) * dh] for g in range(sp * gh, (sp + 1) * gh)],
                          axis=0).astype(BF16) for sp in range(ROW_GROUPS)]
    t_row = qs + (lax.broadcasted_iota(jnp.int32, (rg, 1), 0) & (tq - 1))

    wlen = wbias_ref.shape[2]
    w0 = pl.multiple_of(jnp.maximum(qs + tq - wlen, 0), tq)
    band_bias = wbias_ref[jnp.minimum(i, wbias_ref.shape[0] - 1)]
    kcb = kcmp_ref[0].astype(BF16)
    vcb = vcmp_ref[0].astype(BF16)
    vis_c = (lax.broadcasted_iota(jnp.int32, (rg, n_cmp), 1) * CMP_STRIDE + (CMP_BLOCK - 1)) <= t_row

    def win_probs(sp):
        m_w = jnp.max(sw_ref[sp], axis=1, keepdims=True)
        return jnp.exp(sw_ref[sp] - m_w).astype(BF16)

    for sp in range(ROW_GROUPS):
        qb_ref[sp] = qb[sp]
    for sp in range(ROW_GROUPS):
        sc_ref[sp] = jnp.where(vis_c, lax.dot_general(qb_ref[sp], kcb, nt, preferred_element_type=F32), NEG_INF)
        sw_ref[sp] = lax.dot_general(qb_ref[sp], kwb_ref[pl.ds(w0, wlen), :], nt,
                                     preferred_element_type=F32) + band_bias
    for sp in range(ROW_GROUPS):
        m_c = jnp.max(sc_ref[sp], axis=1, keepdims=True)
        e_c = jnp.where(vis_c, jnp.exp(sc_ref[sp] - m_c), 0.0)
        pc_ref[sp] = e_c
        den_c = jnp.sum(e_c, axis=1, keepdims=True)
        pc_ref[sp] = pc_ref[sp] / jnp.where(den_c > 0.0, den_c, 1.0)
        oc_ref[sp] = jnp.dot(pc_ref[sp].astype(BF16), vcb, preferred_element_type=F32)
    p_sum = None
    for sp in range(ROW_GROUPS):
        for g in range(gh):
            pg = pc_ref[sp, g * tq:(g + 1) * tq, :]
            p_sum = pg if p_sum is None else p_sum + pg

    imp_t = lax.dot_general(ovt_ref[...], p_sum, nt, precision=lax.Precision.HIGHEST,
                            preferred_element_type=F32)
    blk = lax.broadcasted_iota(jnp.int32, (n_sel, tq), 0)
    cur = (qs + lax.broadcasted_iota(jnp.int32, (n_sel, tq), 1)) // SEL_BLOCK
    age = cur - blk
    score = jnp.where(blk == 0, FORCE_SCORE,
                      jnp.where(age < 0, -1.0, jnp.where(age < N_LOCAL_SEL, FORCE_SCORE, imp_t)))
    parts = [jnp.zeros((n_sel, tq), jnp.int32) for _ in range(4)]
    for j in range(n_sel):
        rj = score[j:j + 1, :]
        parts[j % 4] = parts[j % 4] + jnp.where(blk > j, jnp.where(rj >= score, 1, 0),
                                                jnp.where(rj > score, 1, 0))
    rank = (parts[0] + parts[1]) + (parts[2] + parts[3])
    bias_t = jnp.where(rank < min(N_SEL, n_sel), 0.0, NEG_INF)
    bias_t = jnp.concatenate([bias_t, jnp.zeros((LANE - n_sel, tq), F32)], axis=0)
    bias = jnp.concatenate([bias_t.T.astype(BF16)] * gh, axis=0)
    for sp in range(ROW_GROUPS):
        qaug_ref[sp] = jnp.concatenate([qb_ref[sp], bias], axis=1)

    def scores(sp, k0):
        return lax.dot_general(qaug_ref[sp], kaug_ref[pl.ds(k0, KEY_CHUNK), :], nt,
                               preferred_element_type=F32)

    def accumulate(sp, p, alpha, k0):
        pv = jnp.dot(p, vsaug_ref[pl.ds(k0, KEY_CHUNK), :], preferred_element_type=F32)
        acc_ref[sp] = alpha * acc_ref[sp] + pv

    def soften(sp, s_buf):
        m_old = m_ref[sp]
        m_new = jnp.maximum(m_old, jnp.max(s_buf[sp], axis=1, keepdims=True))
        m_ref[sp] = m_new
        return jnp.exp(m_old - m_new), jnp.exp(s_buf[sp] - m_new).astype(BF16)

    n_full = qs // KEY_CHUNK
    k_diag = pl.multiple_of(n_full * KEY_CHUNK, KEY_CHUNK)

    def step(c, s_in, p_prev, s_out, p_out):
        k_next = pl.multiple_of((c + 1) * KEY_CHUNK, KEY_CHUNK)
        k_prev = pl.multiple_of(jnp.where(c == 0, n_full, c - 1) * KEY_CHUNK, KEY_CHUNK)
        for sp in range(ROW_GROUPS):
            accumulate(sp, p_prev[sp], a_ref[sp], k_prev)
            if s_out is not None:
                s_out[sp] = scores(sp, k_next)
        for sp in range(ROW_GROUPS):
            a_ref[sp], p_out[sp] = soften(sp, s_in)

    causal_bias = dbias_ref[i % (KEY_CHUNK // tq)]
    for sp in range(ROW_GROUPS):
        m_ref[sp] = jnp.full((rg, 1), NEG_INF, F32)
        acc_ref[sp] = jnp.zeros((rg, dh + LANE), F32)
        s_b[sp] = scores(sp, k_diag) + causal_bias
        a_ref[sp], p_b[sp] = soften(sp, s_b)
        s_a[sp] = scores(sp, 0)

    for sp in range(ROW_GROUPS):
        ew_ref[sp] = win_probs(sp)
    for sp in range(ROW_GROUPS):
        pv = jnp.dot(ew_ref[sp], vwaug_ref[pl.ds(w0, wlen), :], preferred_element_type=F32)
        ow_ref[sp] = pv[:, :dh] / pv[:, dh:]
    gate = jax.nn.sigmoid(gate_ref[...])
    for g in range(GQA):
        sp, gg = divmod(g, gh)
        sl = slice(gg * tq, (gg + 1) * tq)
        o_ref[:, g * dh:(g + 1) * dh] = (gate[:, N_BRANCH * g:N_BRANCH * g + 1] * oc_ref[sp, sl, :]
                                         + gate[:, N_BRANCH * g + 2:N_BRANCH * g + 3] * ow_ref[sp, sl, :])

    def pair(k, carry):
        step(2 * k, s_a, p_b, s_b, p_a)
        step(2 * k + 1, s_b, p_a, s_a, p_b)
        return carry

    n_steps = jnp.maximum(n_full - 1, 0)
    lax.fori_loop(0, n_steps // 2, pair, 0)
    k_last = pl.multiple_of(n_steps * KEY_CHUNK, KEY_CHUNK)

    @pl.when(n_full == 0)
    def _():
        for sp in range(ROW_GROUPS):
            accumulate(sp, p_b[sp], a_ref[sp], k_diag)

    @pl.when((n_full > 0) & (n_steps % 2 == 0))
    def _():
        step(n_steps, s_a, p_b, None, p_a)
        for sp in range(ROW_GROUPS):
            accumulate(sp, p_a[sp], a_ref[sp], k_last)

    @pl.when((n_full > 0) & (n_steps % 2 == 1))
    def _():
        step(n_steps - 1, s_a, p_b, s_b, p_a)
        step(n_steps, s_b, p_a, None, p_b)
        for sp in range(ROW_GROUPS):
            accumulate(sp, p_b[sp], a_ref[sp], k_last)

    gate = jax.nn.sigmoid(gate_ref[...])
    for g in range(GQA):
        sp, gg = divmod(g, gh)
        acc = acc_ref[sp, gg * tq:(gg + 1) * tq, :]
        o_ref[:, g * dh:(g + 1) * dh] += gate[:, N_BRANCH * g + 1:N_BRANCH * g + 2] * (acc[:, :dh] / acc[:, dh:])


def _attention(q, kv4, gates, k_cmp, v_cmp, ov_t, bsz, s, cast=()):
    dh = HEAD_DIM
    tq = Q_BLOCK
    nq = s // tq
    rg = (GQA // ROW_GROUPS) * tq
    n_cmp = k_cmp.shape[1]
    wlen = min(WINDOW + tq, s)
    kv = lambda blk: pl.BlockSpec((s, dh), lambda b, h, i: (b, blk + h))
    cmp_spec = pl.BlockSpec((1, n_cmp, dh), lambda b, h, i: (b * N_KV_HEADS + h, 0, 0))
    cast_specs, cast_shapes = _cast_rider_specs(cast, bsz * N_KV_HEADS * nq,
                                                lambda b, h, i: ((b * N_KV_HEADS + h) * nq + i, 0))
    return pl.pallas_call(
        functools.partial(_attn_kernel, n_cast=len(cast)),
        grid=(bsz, N_KV_HEADS, nq),
        in_specs=[pl.BlockSpec((tq, GQA * dh), lambda b, h, i: (b * nq + i, h)),
                  kv(0), kv(N_KV_HEADS), kv(2 * N_KV_HEADS), kv(3 * N_KV_HEADS),
                  pl.BlockSpec((tq, LANE), lambda b, h, i: (b * nq + i, h)),
                  cmp_spec, cmp_spec,
                  pl.BlockSpec(ov_t.shape, lambda b, h, i: (0, 0)), *cast_specs],
        out_specs=[pl.BlockSpec((tq, GQA * dh), lambda b, h, i: (b * nq + i, h)), *cast_specs],
        out_shape=[jax.ShapeDtypeStruct((bsz * s, N_HEADS * dh), F32), *cast_shapes],
        scratch_shapes=[pltpu.VMEM((s, dh + LANE), BF16),
                        pltpu.VMEM((s, dh + LANE), BF16),
                        pltpu.VMEM((s, dh), BF16),
                        pltpu.VMEM((s, dh + LANE), BF16),
                        pltpu.VMEM((KEY_CHUNK // tq, rg, KEY_CHUNK), F32),
                        pltpu.VMEM((WINDOW // tq + 1, rg, wlen), F32),
                        pltpu.VMEM((ROW_GROUPS, rg, dh + LANE), BF16),
                        pltpu.VMEM((ROW_GROUPS, rg, KEY_CHUNK), F32),
                        pltpu.VMEM((ROW_GROUPS, rg, KEY_CHUNK), F32),
                        pltpu.VMEM((ROW_GROUPS, rg, KEY_CHUNK), BF16),
                        pltpu.VMEM((ROW_GROUPS, rg, KEY_CHUNK), BF16),
                        pltpu.VMEM((ROW_GROUPS, rg, dh + LANE), F32),
                        pltpu.VMEM((ROW_GROUPS, rg, 1), F32),
                        pltpu.VMEM((ROW_GROUPS, rg, 1), F32),
                        pltpu.VMEM((ROW_GROUPS, rg, dh), BF16),
                        pltpu.VMEM((ROW_GROUPS, rg, n_cmp), F32),
                        pltpu.VMEM((ROW_GROUPS, rg, n_cmp), F32),
                        pltpu.VMEM((ROW_GROUPS, rg, dh), F32),
                        pltpu.VMEM((ROW_GROUPS, rg, wlen), F32),
                        pltpu.VMEM((ROW_GROUPS, rg, wlen), BF16),
                        pltpu.VMEM((ROW_GROUPS, rg, dh), F32)],
        compiler_params=_cparams(("arbitrary", "arbitrary", "arbitrary")),
        name="nsa_attention",
    )(q, kv4, kv4, kv4, kv4, gates, k_cmp, v_cmp, ov_t, *cast)


def _outproj_kernel(lru_ref, attn_ref, ga_ref, x_ref, w_ref, gf_ref, *rest, n_cast):
    cast_in, (h_ref, xn_ref), cast_out = rest[:n_cast], rest[n_cast:n_cast + 2], rest[n_cast + 2:]
    for src, dst in zip(cast_in, cast_out):
        dst[...] = src[...].astype(dst.dtype)
    c = lru_ref.shape[1]
    an = _rms(attn_ref[...], ga_ref[...]).astype(BF16)
    h = (x_ref[...]
         + jnp.dot(lru_ref[...], w_ref[0:c, :], preferred_element_type=F32)
         + jnp.dot(an, w_ref[c:, :], preferred_element_type=F32))
    h_ref[...] = h
    xn_ref[...] = _rms(h, gf_ref[...]).astype(BF16)


def _out_proj(lru_n, attn, g_attn, x2, w_out, g_ffn, tm, cast=()):
    t, d = x2.shape
    c = lru_n.shape[1]
    cast_specs, cast_shapes = _cast_rider_specs(cast, t // tm, lambda i: (i, 0))
    return pl.pallas_call(
        functools.partial(_outproj_kernel, n_cast=len(cast)),
        grid=(t // tm,),
        in_specs=[pl.BlockSpec((tm, c), lambda i: (i, 0)),
                  pl.BlockSpec((tm, attn.shape[1]), lambda i: (i, 0)),
                  pl.BlockSpec((1, attn.shape[1]), lambda i: (0, 0)),
                  pl.BlockSpec((tm, d), lambda i: (i, 0)),
                  pl.BlockSpec(w_out.shape, lambda i: (0, 0), pipeline_mode=pl.Buffered(1)),
                  pl.BlockSpec((1, d), lambda i: (0, 0)), *cast_specs],
        out_specs=[pl.BlockSpec((tm, d), lambda i: (i, 0)),
                   pl.BlockSpec((tm, d), lambda i: (i, 0)), *cast_specs],
        out_shape=[jax.ShapeDtypeStruct((t, d), F32), jax.ShapeDtypeStruct((t, d), BF16), *cast_shapes],
        compiler_params=_cparams(("arbitrary",)),
        name="out_proj",
    )(lru_n, attn, g_attn, x2, w_out, g_ffn, *cast)


def _ffn_kernel(xn_ref, wu_ref, wv_ref, cw_ref, cb_ref, wd_ref, h_ref, g_ref, o_ref,
                halo_ref, upad_ref, *, tiles_per_seq, final_norm):
    i = pl.program_id(0)
    j = pl.program_id(1)
    tm = xn_ref.shape[0]

    @pl.when(j == 0)
    def _():
        o_ref[...] = h_ref[...]

    xn = xn_ref[...]
    u = jnp.dot(xn, wu_ref[...], preferred_element_type=F32)
    v = jnp.dot(xn, wv_ref[...], preferred_element_type=F32)
    first = (i % tiles_per_seq) == 0
    upad_ref[0:SUBLANE, :] = jnp.where(first, 0.0, halo_ref[j])
    upad_ref[SUBLANE:, :] = u
    halo_ref[j] = u[tm - SUBLANE:, :]
    uc = cb_ref[...] + cw_ref[FFN_CONV - 1:FFN_CONV, :] * u
    for k in range(FFN_CONV - 1):
        off = SUBLANE - (FFN_CONV - 1) + k
        uc = uc + cw_ref[k:k + 1, :] * upad_ref[off:off + tm, :]
    act = (jax.nn.gelu(uc) * v).astype(BF16)
    o_ref[...] += jnp.dot(act, wd_ref[...], preferred_element_type=F32)

    if final_norm:
        @pl.when(j == pl.num_programs(1) - 1)
        def _():
            o_ref[...] = _rms(o_ref[...], g_ref[...])


def _ffn(xn, h1, w_up, cw, cb, w_down, g_final, final_norm, s, tm, tf):
    t, d = xn.shape
    dff = w_down.shape[0]
    nj = dff // tf
    return pl.pallas_call(
        functools.partial(_ffn_kernel, tiles_per_seq=s // tm, final_norm=final_norm),
        grid=(t // tm, nj),
        in_specs=[pl.BlockSpec((tm, d), lambda i, j: (i, 0)),
                  pl.BlockSpec((d, tf), lambda i, j: (0, j)),
                  pl.BlockSpec((d, tf), lambda i, j: (0, nj + j)),
                  pl.BlockSpec((FFN_CONV, tf), lambda i, j: (0, j)),
                  pl.BlockSpec((1, tf), lambda i, j: (0, j)),
                  pl.BlockSpec((tf, d), lambda i, j: (j, 0)),
                  pl.BlockSpec((tm, d), lambda i, j: (i, 0)),
                  pl.BlockSpec((1, d), lambda i, j: (0, 0))],
        out_specs=pl.BlockSpec((tm, d), lambda i, j: (i, 0)),
        out_shape=jax.ShapeDtypeStruct((t, d), F32),
        scratch_shapes=[pltpu.VMEM((nj, SUBLANE, tf), F32),
                        pltpu.VMEM((tm + SUBLANE, tf), F32)],
        compiler_params=_cparams(("arbitrary", "arbitrary")),
        name="conv_ffn",
    )(xn, w_up, w_up, cw, cb, w_down, h1, g_final)


def _overlap_t(n_cmp, n_sel):
    c0 = jnp.arange(n_cmp)[None, :] * CMP_STRIDE
    s0 = jnp.arange(n_sel)[:, None] * SEL_BLOCK
    ov = jnp.clip(jnp.minimum(c0 + CMP_BLOCK, s0 + SEL_BLOCK) - jnp.maximum(c0, s0), 0)
    return ov.astype(F32) / CMP_BLOCK


def kernel(x, g_mix, w_in, lru_conv_w, lru_conv_b, lru_wa, lru_ba, lru_wx, lru_bx, lru_lambda, cmp_pe_k, cmp_w1_k, cmp_b1_k, cmp_w2_k, cmp_pe_v, cmp_w1_v, cmp_b1_v, cmp_w2_v, g_lru_out, g_attn_out, w_out, g_ffn, w_up, ffn_conv_w, ffn_conv_b, w_down, g_final):
    bsz, s, d = x.shape
    depth = w_in.shape[0]
    c_lru = lru_conv_w.shape[2]
    attn_w = N_HEADS * HEAD_DIM
    kv_w = N_KV_HEADS * HEAD_DIM
    n_gate = N_BRANCH * GQA
    main_w = 2 * c_lru + attn_w + 6 * kv_w
    segments = ((0, 2 * c_lru, F32, None),
                (2 * c_lru, attn_w, BF16, HEAD_DIM ** -0.5),
                (2 * c_lru + attn_w, 2 * kv_w, F32, None),
                (2 * c_lru + attn_w + 2 * kv_w, 4 * kv_w, BF16, None))
    ov_t = _overlap_t(s // CMP_STRIDE, s // SEL_BLOCK)
    row = lambda a: a.reshape(1, -1)

    h = x.reshape(bsz * s, d)
    for l in range(depth):
        w_main = w_in[l].astype(BF16)
        gpad = jnp.zeros((d, LANE - n_gate), F32)
        w_gate = jnp.concatenate([p for hh in range(N_KV_HEADS)
                                  for p in (w_in[l][:, main_w + hh * n_gate:main_w + (hh + 1) * n_gate], gpad)],
                                 axis=1).astype(BF16)
        lru_in, q, kvc, kv4, gates = _in_proj(h, row(g_mix[l]), w_main, segments, w_gate, tm=512)
        (lru_n,) = _lru(lru_in, bsz, s, lru_conv_w[l], row(lru_conv_b[l]), lru_wa[l].astype(BF16),
                        row(lru_ba[l]), lru_wx[l].astype(BF16), row(lru_bx[l]), row(lru_lambda[l]),
                        row(g_lru_out[l]), ts=256)
        pk = (cmp_pe_k[l].reshape(1, -1), cmp_w1_k[l].astype(BF16), row(cmp_b1_k[l]), cmp_w2_k[l].astype(BF16))
        pv = (cmp_pe_v[l].reshape(1, -1), cmp_w1_v[l].astype(BF16), row(cmp_b1_v[l]), cmp_w2_v[l].astype(BF16))
        k_cmp, v_cmp = _compress(kvc, bsz, s, pk, pv)
        attn, w_out_b, w_down_b, w_up_b = _attention(q, kv4, gates, k_cmp, v_cmp, ov_t, bsz, s,
                                                     cast=(w_out[l], w_down[l], w_up[l]))
        h1, xn = _out_proj(lru_n, attn, row(g_attn_out[l]), h, w_out_b, row(g_ffn[l]), tm=512)
        h = _ffn(xn, h1, w_up_b, ffn_conv_w[l], row(ffn_conv_b[l]), w_down_b,
                 row(g_final), l == depth - 1, s, tm=512, tf=1024)
    return h.reshape(bsz, s, d)
```

```python
import functools

import jax
import jax.numpy as jnp
from jax import lax
from jax.experimental import pallas as pl
from jax.experimental.pallas import tpu as pltpu

F32 = jnp.float32
BF16 = jnp.bfloat16

LANE = 128
SUBLANE = 8
VMEM_LIMIT = 56 * 1024 * 1024

LRU_HEADS = 8
LRU_CONV = 4
LRU_C = 8.0
N_HEADS = 8
N_KV_HEADS = 2
GQA = N_HEADS // N_KV_HEADS
HEAD_DIM = 128
N_BRANCH = 3
CMP_BLOCK = 32
CMP_STRIDE = 16
SEL_BLOCK = 64
N_SEL = 16
N_LOCAL_SEL = 2
WINDOW = 512
Q_BLOCK = 256
FFN_CONV = 3
EPS = 1e-6
NEG_INF = -1e30
FORCE_SCORE = 1e4
LOG2_E = 1.4426950408889634


def _rms(x, g):
    return x * lax.rsqrt(jnp.mean(x * x, axis=-1, keepdims=True) + EPS) * g


def _cparams(sem):
    return pltpu.CompilerParams(dimension_semantics=sem, vmem_limit_bytes=VMEM_LIMIT)


def _inproj_kernel(x_ref, g_ref, w_ref, wg_ref, *o_refs, segments):
    xn = _rms(x_ref[...], g_ref[...]).astype(BF16)
    for (c0, width, scale), o_ref in zip(segments, o_refs):
        y = jnp.dot(xn, w_ref[:, c0:c0 + width], preferred_element_type=F32)
        o_ref[...] = (y if scale is None else y * scale).astype(o_ref.dtype)
    o_refs[-1][...] = jnp.dot(xn, wg_ref[...], preferred_element_type=F32)


def _in_proj(x2, g, w_all, segments, w_gate, tm):
    t, d = x2.shape
    ng = w_gate.shape[1]
    n = max(c0 + width for c0, width, _, _ in segments)
    resident = lambda shape: pl.BlockSpec(shape, lambda i: (0, 0), pipeline_mode=pl.Buffered(1))
    return pl.pallas_call(
        functools.partial(_inproj_kernel, segments=tuple((c0, width, sc) for c0, width, _, sc in segments)),
        grid=(t // tm,),
        in_specs=[pl.BlockSpec((tm, d), lambda i: (i, 0)),
                  pl.BlockSpec((1, d), lambda i: (0, 0)),
                  resident((d, n)), resident(w_gate.shape)],
        out_specs=[*[pl.BlockSpec((tm, width), lambda i: (i, 0)) for _, width, _, _ in segments],
                   pl.BlockSpec((tm, ng), lambda i: (i, 0))],
        out_shape=[*[jax.ShapeDtypeStruct((t, width), dt) for _, width, dt, _ in segments],
                   jax.ShapeDtypeStruct((t, ng), F32)],
        compiler_params=_cparams(("arbitrary",)),
        name="in_proj",
    )(x2, g, w_all, w_gate)


def _cast_rider_specs(arrays, n_steps, index):
    specs, shapes = [], []
    for a in arrays:
        rows = a.shape[0] // n_steps
        assert rows * n_steps == a.shape[0] and rows % (2 * SUBLANE) == 0
        specs.append(pl.BlockSpec((rows, a.shape[1]), index))
        shapes.append(jax.ShapeDtypeStruct(a.shape, BF16))
    return specs, shapes


def _lru_kernel(x_ref, gate_ref, cw_ref, cb_ref, wa_ref, ba_ref, wx_ref, bx_ref, lam_ref, g_ref, *rest, n_cast):
    cast_in, o_ref, cast_out = rest[:n_cast], rest[n_cast], rest[n_cast + 1:2 * n_cast + 1]
    xpad_ref, carry_ref, a_ref, b_ref = rest[2 * n_cast + 1:]
    for src, dst in zip(cast_in, cast_out):
        dst[...] = src[...].astype(dst.dtype)
    ts, c = x_ref.shape
    hb = c // LRU_HEADS

    @pl.when(pl.program_id(1) == 0)
    def _():
        xpad_ref[...] = jnp.zeros_like(xpad_ref)
        carry_ref[...] = jnp.zeros_like(carry_ref)

    xpad_ref[0:SUBLANE, :] = xpad_ref[ts:ts + SUBLANE, :]
    xpad_ref[SUBLANE:, :] = x_ref[...]
    u = cb_ref[...] + cw_ref[LRU_CONV - 1:LRU_CONV, :] * x_ref[...]
    for k in range(LRU_CONV - 1):
        off = SUBLANE - (LRU_CONV - 1) + k
        u = u + cw_ref[k:k + 1, :] * xpad_ref[off:off + ts, :]

    ub = u.astype(BF16)
    r_parts, i_parts = [], []
    for h in range(LRU_HEADS):
        uh = ub[:, h * hb:(h + 1) * hb]
        r_parts.append(jnp.dot(uh, wa_ref[h], preferred_element_type=F32))
        i_parts.append(jnp.dot(uh, wx_ref[h], preferred_element_type=F32))
    r = jax.nn.sigmoid(jnp.concatenate(r_parts, axis=1) + ba_ref[...])
    ig = jax.nn.sigmoid(jnp.concatenate(i_parts, axis=1) + bx_ref[...])
    nl = -lam_ref[...]
    softplus = jnp.maximum(nl, 0.0) + jnp.log1p(jnp.exp(-jnp.abs(nl)))
    log_a = (-LRU_C) * r * softplus
    a = jnp.exp(log_a)
    a_ref[...] = a
    z = -jnp.tanh(log_a) * (a * a + 1.0)
    root = jnp.where(z > 0.0, z * lax.rsqrt(z), 0.0)
    b_ref[...] = root * (ig * u)

    row = lax.broadcasted_iota(jnp.int32, (SUBLANE, c), 0)

    def group(gi, carry):
        r0 = pl.multiple_of(gi * SUBLANE, SUBLANE)
        a8 = a_ref[pl.ds(r0, SUBLANE), :]
        b8 = b_ref[pl.ds(r0, SUBLANE), :]
        for d in (1, 2, 4):
            a_sh = pltpu.roll(a8, d, 0)
            b_sh = pltpu.roll(b8, d, 0)
            m = row >= d
            b8 = jnp.where(m, a8 * b_sh + b8, b8)
            a8 = jnp.where(m, a8 * a_sh, a8)
        h8 = a8 * carry + b8
        b_ref[pl.ds(r0, SUBLANE), :] = h8
        return jnp.broadcast_to(h8[SUBLANE - 1:SUBLANE, :], (SUBLANE, c))

    carry_ref[...] = lax.fori_loop(0, ts // SUBLANE, group, carry_ref[...])

    y = b_ref[...] * jax.nn.gelu(gate_ref[...])
    o_ref[...] = _rms(y, g_ref[...]).astype(o_ref.dtype)


def _lru(proj, bsz, s, cw, cb, wa, ba, wx, bx, lam, g, ts, cast=()):
    c = cw.shape[1]
    ns = s // ts
    row = lambda b, i: b * ns + i
    vec = pl.BlockSpec((1, c), lambda b, i: (0, 0))
    mat = pl.BlockSpec(wa.shape, lambda b, i: (0, 0, 0))
    cast_specs, cast_shapes = _cast_rider_specs(cast, bsz * ns, lambda b, i: (row(b, i), 0))
    return pl.pallas_call(
        functools.partial(_lru_kernel, n_cast=len(cast)),
        grid=(bsz, ns),
        in_specs=[pl.BlockSpec((ts, c), lambda b, i: (row(b, i), 0)),
                  pl.BlockSpec((ts, c), lambda b, i: (row(b, i), 1)),
                  pl.BlockSpec((LRU_CONV, c), lambda b, i: (0, 0)), vec,
                  mat, vec, mat, vec, vec, vec, *cast_specs],
        out_specs=[pl.BlockSpec((ts, c), lambda b, i: (row(b, i), 0)), *cast_specs],
        out_shape=[jax.ShapeDtypeStruct((bsz * s, c), BF16), *cast_shapes],
        scratch_shapes=[pltpu.VMEM((ts + SUBLANE, c), F32),
                        pltpu.VMEM((SUBLANE, c), F32),
                        pltpu.VMEM((ts, c), F32),
                        pltpu.VMEM((ts, c), F32)],
        compiler_params=_cparams(("arbitrary", "arbitrary")),
        name="rg_lru",
    )(proj, proj, cw, cb, wa, ba, wx, bx, lam, g, *cast)


def _compress_one(kv_ref, pe_ref, w1_ref, b1_ref, w2_ref):
    s, dh = kv_ref.shape
    n = s // CMP_STRIDE
    half = CMP_STRIDE * dh
    xa = jnp.concatenate([kv_ref[pl.ds(l, n, stride=CMP_STRIDE), :] for l in range(CMP_STRIDE)],
                         axis=1).astype(BF16)
    lo = jnp.dot(xa, w1_ref[0:half, :], preferred_element_type=F32)
    hi = jnp.dot(xa, w1_ref[half:2 * half, :], preferred_element_type=F32)
    pe = jnp.broadcast_to(pe_ref[...], (SUBLANE, 2 * half)).astype(BF16)
    pe_term = jnp.dot(pe, w1_ref[...], preferred_element_type=F32)[0:1, :]
    pre = lo + pltpu.roll(hi, n - 1, 0) + pe_term + b1_ref[...]
    return jnp.dot(jax.nn.gelu(pre).astype(BF16), w2_ref[...], preferred_element_type=F32)


def _compress_kernel(kc_ref, vc_ref, pek_ref, w1k_ref, b1k_ref, w2k_ref,
                     pev_ref, w1v_ref, b1v_ref, w2v_ref, ko_ref, vo_ref):
    ko_ref[0] = _compress_one(kc_ref, pek_ref, w1k_ref, b1k_ref, w2k_ref)
    vo_ref[0] = _compress_one(vc_ref, pev_ref, w1v_ref, b1v_ref, w2v_ref)


def _compress(kvc, bsz, s, pk, pv):
    dh = HEAD_DIM
    kc_blk, vc_blk = 0, N_KV_HEADS
    n = s // CMP_STRIDE
    full = lambda a: pl.BlockSpec(a.shape, lambda b, h: (0,) * a.ndim)
    out = pl.BlockSpec((1, n, dh), lambda b, h: (b * N_KV_HEADS + h, 0, 0))
    shp = jax.ShapeDtypeStruct((bsz * N_KV_HEADS, n, dh), F32)
    return pl.pallas_call(
        _compress_kernel,
        grid=(bsz, N_KV_HEADS),
        in_specs=[pl.BlockSpec((s, dh), lambda b, h: (b, kc_blk + h)),
                  pl.BlockSpec((s, dh), lambda b, h: (b, vc_blk + h)),
                  *[full(a) for a in pk], *[full(a) for a in pv]],
        out_specs=[out, out],
        out_shape=[shp, shp],
        compiler_params=_cparams(("arbitrary", "arbitrary")),
        name="compress_kv",
    )(kvc, kvc, *pk, *pv)


KEY_CHUNK = 512
ROW_GROUPS = 2


def _attn_kernel(q_ref, ksel_ref, vsel_ref, kw_ref, vw_ref, gate_ref, kcmp_ref, vcmp_ref, ovt_ref, *rest, n_cast):
    cast_in, o_ref, cast_out = rest[:n_cast], rest[n_cast], rest[n_cast + 1:2 * n_cast + 1]
    (kaug_ref, vsaug_ref, kwb_ref, vwaug_ref, dbias_ref, wbias_ref,
     qaug_ref, s_a, s_b, p_a, p_b, acc_ref, m_ref, a_ref,
     qb_ref, sc_ref, pc_ref, oc_ref, sw_ref, ew_ref, ow_ref) = rest[2 * n_cast + 1:]
    for src, dst in zip(cast_in, cast_out):
        dst[...] = src[...].astype(dst.dtype)
    i = pl.program_id(2)
    s_len, dh = ksel_ref.shape
    tq = q_ref.shape[0]
    n_cmp = kcmp_ref.shape[1]
    n_sel = s_len // SEL_BLOCK
    gh = GQA // ROW_GROUPS
    rg = gh * tq

    @pl.when(i == 0)
    def _():
        kaug_ref[:, 0:dh] = ksel_ref[...].astype(BF16)
        kblk = lax.broadcasted_iota(jnp.int32, (s_len, LANE), 0) // SEL_BLOCK
        lane = lax.broadcasted_iota(jnp.int32, (s_len, LANE), 1)
        kaug_ref[:, dh:dh + LANE] = jnp.where(kblk == lane, 1.0, 0.0).astype(BF16)
        ones = jnp.ones((s_len, LANE), BF16)
        vsaug_ref[:, 0:dh] = vsel_ref[...].astype(BF16)
        vsaug_ref[:, dh:dh + LANE] = ones
        kwb_ref[...] = kw_ref[...].astype(BF16)
        vwaug_ref[:, 0:dh] = vw_ref[...].astype(BF16)
        vwaug_ref[:, dh:dh + LANE] = ones
        for v in range(dbias_ref.shape[0]):
            col = lax.broadcasted_iota(jnp.int32, (rg, KEY_CHUNK), 1)
            tok = lax.broadcasted_iota(jnp.int32, (rg, KEY_CHUNK), 0) & (tq - 1)
            dbias_ref[v] = jnp.where(col <= v * tq + tok, 0.0, NEG_INF)
        for v in range(wbias_ref.shape[0]):
            col = lax.broadcasted_iota(jnp.int32, wbias_ref.shape[1:], 1)
            tok = lax.broadcasted_iota(jnp.int32, wbias_ref.shape[1:], 0) & (tq - 1)
            d = v * tq + tok - col
            wbias_ref[v] = jnp.where(d >= 0, jnp.where(d < WINDOW, 0.0, NEG_INF), NEG_INF)

    nt = (((1,), (1,)), ((), ()))
    qs = i * tq
    qf = q_ref[...]
    qb = [jnp.concatenate([qf[:, g * dh:(g + 1) * dh] for g in range(sp * gh, (sp + 1) * gh)],
                          axis=0).astype(BF16) for sp in range(ROW_GROUPS)]
    t_row = qs + (lax.broadcasted_iota(jnp.int32, (rg, 1), 0) & (tq - 1))

    wlen = wbias_ref.shape[2]
    w0 = pl.multiple_of(jnp.maximum(qs + tq - wlen, 0), tq)
    band_bias = wbias_ref[jnp.minimum(i, wbias_ref.shape[0] - 1)]
    kcb = kcmp_ref[0].astype(BF16)
    vcb = vcmp_ref[0].astype(BF16)
    vis_c = (lax.broadcasted_iota(jnp.int32, (rg, n_cmp), 1) * CMP_STRIDE + (CMP_BLOCK - 1)) <= t_row

    def win_probs(sp):
        m_w = jnp.max(sw_ref[sp], axis=1, keepdims=True)
        return jnp.exp2(sw_ref[sp] - m_w).astype(BF16)

    for sp in range(ROW_GROUPS):
        qb_ref[sp] = qb[sp]
    for sp in range(ROW_GROUPS):
        sc_ref[sp] = jnp.where(vis_c, lax.dot_general(qb_ref[sp], kcb, nt, preferred_element_type=F32), NEG_INF)
        sw_ref[sp] = lax.dot_general(qb_ref[sp], kwb_ref[pl.ds(w0, wlen), :], nt,
                                     preferred_element_type=F32) + band_bias
    for sp in range(ROW_GROUPS):
        m_c = jnp.max(sc_ref[sp], axis=1, keepdims=True)
        e_c = jnp.where(vis_c, jnp.exp2(sc_ref[sp] - m_c), 0.0)
        pc_ref[sp] = e_c
        den_c = jnp.sum(e_c, axis=1, keepdims=True)
        pc_ref[sp] = pc_ref[sp] / jnp.where(den_c > 0.0, den_c, 1.0)
        oc_ref[sp] = jnp.dot(pc_ref[sp].astype(BF16), vcb, preferred_element_type=F32)
    p_sum = None
    for sp in range(ROW_GROUPS):
        for g in range(gh):
            pg = pc_ref[sp, g * tq:(g + 1) * tq, :]
            p_sum = pg if p_sum is None else p_sum + pg

    imp_t = lax.dot_general(ovt_ref[...], p_sum, nt, precision=lax.Precision.HIGHEST,
                            preferred_element_type=F32)
    blk = lax.broadcasted_iota(jnp.int32, (n_sel, tq), 0)
    cur = (qs + lax.broadcasted_iota(jnp.int32, (n_sel, tq), 1)) // SEL_BLOCK
    age = cur - blk
    score = jnp.where(blk == 0, FORCE_SCORE,
                      jnp.where(age < 0, -1.0, jnp.where(age < N_LOCAL_SEL, FORCE_SCORE, imp_t)))
    parts = [jnp.zeros((n_sel, tq), jnp.int32) for _ in range(4)]
    for j in range(n_sel):
        rj = score[j:j + 1, :]
        parts[j % 4] = parts[j % 4] + jnp.where(blk > j, jnp.where(rj >= score, 1, 0),
                                                jnp.where(rj > score, 1, 0))
    rank = (parts[0] + parts[1]) + (parts[2] + parts[3])
    bias_t = jnp.where(rank < min(N_SEL, n_sel), 0.0, NEG_INF)
    bias_t = jnp.concatenate([bias_t, jnp.zeros((LANE - n_sel, tq), F32)], axis=0)
    bias = jnp.concatenate([bias_t.T.astype(BF16)] * gh, axis=0)
    for sp in range(ROW_GROUPS):
        qaug_ref[sp] = jnp.concatenate([qb_ref[sp], bias], axis=1)

    def scores(sp, k0):
        return lax.dot_general(qaug_ref[sp], kaug_ref[pl.ds(k0, KEY_CHUNK), :], nt,
                               preferred_element_type=F32)

    def accumulate(sp, p, alpha, k0):
        pv = jnp.dot(p, vsaug_ref[pl.ds(k0, KEY_CHUNK), :], preferred_element_type=F32)
        acc_ref[sp] = alpha * acc_ref[sp] + pv

    def soften(sp, s_buf):
        m_old = m_ref[sp]
        m_new = jnp.maximum(m_old, jnp.max(s_buf[sp], axis=1, keepdims=True))
        m_ref[sp] = m_new
        return jnp.exp2(m_old - m_new), jnp.exp2(s_buf[sp] - m_new).astype(BF16)

    n_full = qs // KEY_CHUNK
    k_diag = pl.multiple_of(n_full * KEY_CHUNK, KEY_CHUNK)

    def step(c, s_in, p_prev, s_out, p_out):
        k_next = pl.multiple_of((c + 1) * KEY_CHUNK, KEY_CHUNK)
        k_prev = pl.multiple_of(jnp.where(c == 0, n_full, c - 1) * KEY_CHUNK, KEY_CHUNK)
        for sp in range(ROW_GROUPS):
            accumulate(sp, p_prev[sp], a_ref[sp], k_prev)
            if s_out is not None:
                s_out[sp] = scores(sp, k_next)
        for sp in range(ROW_GROUPS):
            a_ref[sp], p_out[sp] = soften(sp, s_in)

    causal_bias = dbias_ref[i % (KEY_CHUNK // tq)]
    for sp in range(ROW_GROUPS):
        m_ref[sp] = jnp.full((rg, 1), NEG_INF, F32)
        acc_ref[sp] = jnp.zeros((rg, dh + LANE), F32)
        s_b[sp] = scores(sp, k_diag) + causal_bias
        a_ref[sp], p_b[sp] = soften(sp, s_b)
        s_a[sp] = scores(sp, 0)

    for sp in range(ROW_GROUPS):
        ew_ref[sp] = win_probs(sp)
    for sp in range(ROW_GROUPS):
        pv = jnp.dot(ew_ref[sp], vwaug_ref[pl.ds(w0, wlen), :], preferred_element_type=F32)
        ow_ref[sp] = pv[:, :dh] / pv[:, dh:]
    gate = jax.nn.sigmoid(gate_ref[...])
    for g in range(GQA):
        sp, gg = divmod(g, gh)
        sl = slice(gg * tq, (gg + 1) * tq)
        o_ref[:, g * dh:(g + 1) * dh] = (gate[:, N_BRANCH * g:N_BRANCH * g + 1] * oc_ref[sp, sl, :]
                                         + gate[:, N_BRANCH * g + 2:N_BRANCH * g + 3] * ow_ref[sp, sl, :])

    def pair(k, carry):
        step(2 * k, s_a, p_b, s_b, p_a)
        step(2 * k + 1, s_b, p_a, s_a, p_b)
        return carry

    n_steps = jnp.maximum(n_full - 1, 0)
    lax.fori_loop(0, n_steps // 2, pair, 0)
    k_last = pl.multiple_of(n_steps * KEY_CHUNK, KEY_CHUNK)

    @pl.when(n_full == 0)
    def _():
        for sp in range(ROW_GROUPS):
            accumulate(sp, p_b[sp], a_ref[sp], k_diag)

    @pl.when((n_full > 0) & (n_steps % 2 == 0))
    def _():
        step(n_steps, s_a, p_b, None, p_a)
        for sp in range(ROW_GROUPS):
            accumulate(sp, p_a[sp], a_ref[sp], k_last)

    @pl.when((n_full > 0) & (n_steps % 2 == 1))
    def _():
        step(n_steps - 1, s_a, p_b, s_b, p_a)
        step(n_steps, s_b, p_a, None, p_b)
        for sp in range(ROW_GROUPS):
            accumulate(sp, p_b[sp], a_ref[sp], k_last)

    gate = jax.nn.sigmoid(gate_ref[...])
    for g in range(GQA):
        sp, gg = divmod(g, gh)
        acc = acc_ref[sp, gg * tq:(gg + 1) * tq, :]
        o_ref[:, g * dh:(g + 1) * dh] += gate[:, N_BRANCH * g + 1:N_BRANCH * g + 2] * (acc[:, :dh] / acc[:, dh:])


def _attention(q, kv4, gates, k_cmp, v_cmp, ov_t, bsz, s, cast=()):
    dh = HEAD_DIM
    tq = Q_BLOCK
    nq = s // tq
    rg = (GQA // ROW_GROUPS) * tq
    n_cmp = k_cmp.shape[1]
    wlen = min(WINDOW + tq, s)
    kv = lambda blk: pl.BlockSpec((s, dh), lambda b, h, i: (b, blk + h))
    cmp_spec = pl.BlockSpec((1, n_cmp, dh), lambda b, h, i: (b * N_KV_HEADS + h, 0, 0))
    cast_specs, cast_shapes = _cast_rider_specs(cast, bsz * N_KV_HEADS * nq,
                                                lambda b, h, i: ((b * N_KV_HEADS + h) * nq + i, 0))
    return pl.pallas_call(
        functools.partial(_attn_kernel, n_cast=len(cast)),
        grid=(bsz, N_KV_HEADS, nq),
        in_specs=[pl.BlockSpec((tq, GQA * dh), lambda b, h, i: (b * nq + i, h)),
                  kv(0), kv(N_KV_HEADS), kv(2 * N_KV_HEADS), kv(3 * N_KV_HEADS),
                  pl.BlockSpec((tq, LANE), lambda b, h, i: (b * nq + i, h)),
                  cmp_spec, cmp_spec,
                  pl.BlockSpec(ov_t.shape, lambda b, h, i: (0, 0)), *cast_specs],
        out_specs=[pl.BlockSpec((tq, GQA * dh), lambda b, h, i: (b * nq + i, h)), *cast_specs],
        out_shape=[jax.ShapeDtypeStruct((bsz * s, N_HEADS * dh), F32), *cast_shapes],
        scratch_shapes=[pltpu.VMEM((s, dh + LANE), BF16),
                        pltpu.VMEM((s, dh + LANE), BF16),
                        pltpu.VMEM((s, dh), BF16),
                        pltpu.VMEM((s, dh + LANE), BF16),
                        pltpu.VMEM((KEY_CHUNK // tq, rg, KEY_CHUNK), F32),
                        pltpu.VMEM((WINDOW // tq + 1, rg, wlen), F32),
                        pltpu.VMEM((ROW_GROUPS, rg, dh + LANE), BF16),
                        pltpu.VMEM((ROW_GROUPS, rg, KEY_CHUNK), F32),
                        pltpu.VMEM((ROW_GROUPS, rg, KEY_CHUNK), F32),
                        pltpu.VMEM((ROW_GROUPS, rg, KEY_CHUNK), BF16),
                        pltpu.VMEM((ROW_GROUPS, rg, KEY_CHUNK), BF16),
                        pltpu.VMEM((ROW_GROUPS, rg, dh + LANE), F32),
                        pltpu.VMEM((ROW_GROUPS, rg, 1), F32),
                        pltpu.VMEM((ROW_GROUPS, rg, 1), F32),
                        pltpu.VMEM((ROW_GROUPS, rg, dh), BF16),
                        pltpu.VMEM((ROW_GROUPS, rg, n_cmp), F32),
                        pltpu.VMEM((ROW_GROUPS, rg, n_cmp), F32),
                        pltpu.VMEM((ROW_GROUPS, rg, dh), F32),
                        pltpu.VMEM((ROW_GROUPS, rg, wlen), F32),
                        pltpu.VMEM((ROW_GROUPS, rg, wlen), BF16),
                        pltpu.VMEM((ROW_GROUPS, rg, dh), F32)],
        compiler_params=_cparams(("arbitrary", "arbitrary", "arbitrary")),
        name="nsa_attention",
    )(q, kv4, kv4, kv4, kv4, gates, k_cmp, v_cmp, ov_t, *cast)


def _outproj_kernel(lru_ref, attn_ref, ga_ref, x_ref, w_ref, gf_ref, *rest, n_cast):
    cast_in, (h_ref, xn_ref), cast_out = rest[:n_cast], rest[n_cast:n_cast + 2], rest[n_cast + 2:]
    for src, dst in zip(cast_in, cast_out):
        dst[...] = src[...].astype(dst.dtype)
    c = lru_ref.shape[1]
    an = _rms(attn_ref[...], ga_ref[...]).astype(BF16)
    h = (x_ref[...]
         + jnp.dot(lru_ref[...], w_ref[0:c, :], preferred_element_type=F32)
         + jnp.dot(an, w_ref[c:, :], preferred_element_type=F32))
    h_ref[...] = h
    xn_ref[...] = _rms(h, gf_ref[...]).astype(BF16)


def _out_proj(lru_n, attn, g_attn, x2, w_out, g_ffn, tm, cast=()):
    t, d = x2.shape
    c = lru_n.shape[1]
    cast_specs, cast_shapes = _cast_rider_specs(cast, t // tm, lambda i: (i, 0))
    return pl.pallas_call(
        functools.partial(_outproj_kernel, n_cast=len(cast)),
        grid=(t // tm,),
        in_specs=[pl.BlockSpec((tm, c), lambda i: (i, 0)),
                  pl.BlockSpec((tm, attn.shape[1]), lambda i: (i, 0)),
                  pl.BlockSpec((1, attn.shape[1]), lambda i: (0, 0)),
                  pl.BlockSpec((tm, d), lambda i: (i, 0)),
                  pl.BlockSpec(w_out.shape, lambda i: (0, 0), pipeline_mode=pl.Buffered(1)),
                  pl.BlockSpec((1, d), lambda i: (0, 0)), *cast_specs],
        out_specs=[pl.BlockSpec((tm, d), lambda i: (i, 0)),
                   pl.BlockSpec((tm, d), lambda i: (i, 0)), *cast_specs],
        out_shape=[jax.ShapeDtypeStruct((t, d), F32), jax.ShapeDtypeStruct((t, d), BF16), *cast_shapes],
        compiler_params=_cparams(("arbitrary",)),
        name="out_proj",
    )(lru_n, attn, g_attn, x2, w_out, g_ffn, *cast)


def _ffn_kernel(xn_ref, wu_ref, wv_ref, cw_ref, cb_ref, wd_ref, h_ref, g_ref, o_ref,
                halo_ref, upad_ref, *, tiles_per_seq, final_norm):
    i = pl.program_id(0)
    j = pl.program_id(1)
    tm = xn_ref.shape[0]

    @pl.when(j == 0)
    def _():
        o_ref[...] = h_ref[...]

    xn = xn_ref[...]
    u = jnp.dot(xn, wu_ref[...], preferred_element_type=F32)
    v = jnp.dot(xn, wv_ref[...], preferred_element_type=F32)
    first = (i % tiles_per_seq) == 0
    upad_ref[0:SUBLANE, :] = jnp.where(first, 0.0, halo_ref[j])
    upad_ref[SUBLANE:, :] = u
    halo_ref[j] = u[tm - SUBLANE:, :]
    uc = cb_ref[...] + cw_ref[FFN_CONV - 1:FFN_CONV, :] * u
    for k in range(FFN_CONV - 1):
        off = SUBLANE - (FFN_CONV - 1) + k
        uc = uc + cw_ref[k:k + 1, :] * upad_ref[off:off + tm, :]
    act = (jax.nn.gelu(uc) * v).astype(BF16)
    o_ref[...] += jnp.dot(act, wd_ref[...], preferred_element_type=F32)

    if final_norm:
        @pl.when(j == pl.num_programs(1) - 1)
        def _():
            o_ref[...] = _rms(o_ref[...], g_ref[...])


def _ffn(xn, h1, w_up, cw, cb, w_down, g_final, final_norm, s, tm, tf):
    t, d = xn.shape
    dff = w_down.shape[0]
    nj = dff // tf
    return pl.pallas_call(
        functools.partial(_ffn_kernel, tiles_per_seq=s // tm, final_norm=final_norm),
        grid=(t // tm, nj),
        in_specs=[pl.BlockSpec((tm, d), lambda i, j: (i, 0)),
                  pl.BlockSpec((d, tf), lambda i, j: (0, j)),
                  pl.BlockSpec((d, tf), lambda i, j: (0, nj + j)),
                  pl.BlockSpec((FFN_CONV, tf), lambda i, j: (0, j)),
                  pl.BlockSpec((1, tf), lambda i, j: (0, j)),
                  pl.BlockSpec((tf, d), lambda i, j: (j, 0)),
                  pl.BlockSpec((tm, d), lambda i, j: (i, 0)),
                  pl.BlockSpec((1, d), lambda i, j: (0, 0))],
        out_specs=pl.BlockSpec((tm, d), lambda i, j: (i, 0)),
        out_shape=jax.ShapeDtypeStruct((t, d), F32),
        scratch_shapes=[pltpu.VMEM((nj, SUBLANE, tf), F32),
                        pltpu.VMEM((tm + SUBLANE, tf), F32)],
        compiler_params=_cparams(("arbitrary", "arbitrary")),
        name="conv_ffn",
    )(xn, w_up, w_up, cw, cb, w_down, h1, g_final)


def _overlap_t(n_cmp, n_sel):
    c0 = jnp.arange(n_cmp)[None, :] * CMP_STRIDE
    s0 = jnp.arange(n_sel)[:, None] * SEL_BLOCK
    ov = jnp.clip(jnp.minimum(c0 + CMP_BLOCK, s0 + SEL_BLOCK) - jnp.maximum(c0, s0), 0)
    return ov.astype(F32) / CMP_BLOCK


def kernel(x, g_mix, w_in, lru_conv_w, lru_conv_b, lru_wa, lru_ba, lru_wx, lru_bx, lru_lambda, cmp_pe_k, cmp_w1_k, cmp_b1_k, cmp_w2_k, cmp_pe_v, cmp_w1_v, cmp_b1_v, cmp_w2_v, g_lru_out, g_attn_out, w_out, g_ffn, w_up, ffn_conv_w, ffn_conv_b, w_down, g_final):
    bsz, s, d = x.shape
    depth = w_in.shape[0]
    c_lru = lru_conv_w.shape[2]
    attn_w = N_HEADS * HEAD_DIM
    kv_w = N_KV_HEADS * HEAD_DIM
    n_gate = N_BRANCH * GQA
    main_w = 2 * c_lru + attn_w + 6 * kv_w
    segments = ((0, 2 * c_lru, F32, None),
                (2 * c_lru, attn_w, BF16, HEAD_DIM ** -0.5 * LOG2_E),
                (2 * c_lru + attn_w, 2 * kv_w, F32, None),
                (2 * c_lru + attn_w + 2 * kv_w, 4 * kv_w, BF16, None))
    ov_t = _overlap_t(s // CMP_STRIDE, s // SEL_BLOCK)
    row = lambda a: a.reshape(1, -1)

    h = x.reshape(bsz * s, d)
    for l in range(depth):
        w_main = w_in[l].astype(BF16)
        gpad = jnp.zeros((d, LANE - n_gate), F32)
        w_gate = jnp.concatenate([p for hh in range(N_KV_HEADS)
                                  for p in (w_in[l][:, main_w + hh * n_gate:main_w + (hh + 1) * n_gate], gpad)],
                                 axis=1).astype(BF16)
        lru_in, q, kvc, kv4, gates = _in_proj(h, row(g_mix[l]), w_main, segments, w_gate, tm=512)
        (lru_n,) = _lru(lru_in, bsz, s, lru_conv_w[l], row(lru_conv_b[l]), lru_wa[l].astype(BF16),
                        row(lru_ba[l]), lru_wx[l].astype(BF16), row(lru_bx[l]), row(lru_lambda[l]),
                        row(g_lru_out[l]), ts=512)
        pk = (cmp_pe_k[l].reshape(1, -1), cmp_w1_k[l].astype(BF16), row(cmp_b1_k[l]), cmp_w2_k[l].astype(BF16))
        pv = (cmp_pe_v[l].reshape(1, -1), cmp_w1_v[l].astype(BF16), row(cmp_b1_v[l]), cmp_w2_v[l].astype(BF16))
        k_cmp, v_cmp = _compress(kvc, bsz, s, pk, pv)
        attn, w_out_b, w_down_b, w_up_b = _attention(q, kv4, gates, k_cmp, v_cmp, ov_t, bsz, s,
                                                     cast=(w_out[l], w_down[l], w_up[l]))
        h1, xn = _out_proj(lru_n, attn, row(g_attn_out[l]), h, w_out_b, row(g_ffn[l]), tm=512)
        h = _ffn(xn, h1, w_up_b, ffn_conv_w[l], row(ffn_conv_b[l]), w_down_b,
                 row(g_final), l == depth - 1, s, tm=512, tf=1024)
    return h.reshape(bsz, s, d)
```

```python
import functools

import jax
import jax.numpy as jnp
from jax import lax
from jax.experimental import pallas as pl
from jax.experimental.pallas import tpu as pltpu

F32 = jnp.float32
BF16 = jnp.bfloat16

LANE = 128
SUBLANE = 8
VMEM_LIMIT = 56 * 1024 * 1024

LRU_HEADS = 8
LRU_CONV = 4
LRU_C = 8.0
N_HEADS = 8
N_KV_HEADS = 2
GQA = N_HEADS // N_KV_HEADS
HEAD_DIM = 128
N_BRANCH = 3
CMP_BLOCK = 32
CMP_STRIDE = 16
SEL_BLOCK = 64
N_SEL = 16
N_LOCAL_SEL = 2
WINDOW = 512
Q_BLOCK = 256
FFN_CONV = 3
EPS = 1e-6
NEG_INF = -1e30
FORCE_SCORE = 1e4
LOG2_E = 1.4426950408889634

ROW_TILE = 512
FF_TILE = 1024
LRU_TILE = 256


def _rms(x, g):
    return x * lax.rsqrt(jnp.mean(x * x, axis=-1, keepdims=True) + EPS) * g


def _cparams(sem):
    return pltpu.CompilerParams(dimension_semantics=sem, vmem_limit_bytes=VMEM_LIMIT)


def _inproj_kernel(x_ref, g_ref, w_ref, wg_ref, *o_refs, segments):
    xn = _rms(x_ref[...], g_ref[...]).astype(BF16)
    for (c0, width, scale), o_ref in zip(segments, o_refs):
        y = jnp.dot(xn, w_ref[:, c0:c0 + width], preferred_element_type=F32)
        o_ref[...] = (y if scale is None else y * scale).astype(o_ref.dtype)
    o_refs[-1][...] = jnp.dot(xn, wg_ref[...], preferred_element_type=F32)


def _in_proj(x2, g, w_all, segments, w_gate, tm):
    t, d = x2.shape
    ng = w_gate.shape[1]
    n = max(c0 + width for c0, width, _, _ in segments)
    resident = lambda shape: pl.BlockSpec(shape, lambda i: (0, 0), pipeline_mode=pl.Buffered(1))
    return pl.pallas_call(
        functools.partial(_inproj_kernel, segments=tuple((c0, width, sc) for c0, width, _, sc in segments)),
        grid=(t // tm,),
        in_specs=[pl.BlockSpec((tm, d), lambda i: (i, 0)),
                  pl.BlockSpec((1, d), lambda i: (0, 0)),
                  resident((d, n)), resident(w_gate.shape)],
        out_specs=[*[pl.BlockSpec((tm, width), lambda i: (i, 0)) for _, width, _, _ in segments],
                   pl.BlockSpec((tm, ng), lambda i: (i, 0))],
        out_shape=[*[jax.ShapeDtypeStruct((t, width), dt) for _, width, dt, _ in segments],
                   jax.ShapeDtypeStruct((t, ng), F32)],
        compiler_params=_cparams(("arbitrary",)),
        name="in_proj",
    )(x2, g, w_all, w_gate)


def _cast_rider_specs(arrays, n_steps, index):
    specs, shapes = [], []
    for a in arrays:
        rows = a.shape[0] // n_steps
        assert rows * n_steps == a.shape[0] and rows % (2 * SUBLANE) == 0
        specs.append(pl.BlockSpec((rows, a.shape[1]), index))
        shapes.append(jax.ShapeDtypeStruct(a.shape, BF16))
    return specs, shapes


def _lru_kernel(x_ref, gate_ref, cw_ref, cb_ref, wa_ref, ba_ref, wx_ref, bx_ref, lam_ref, g_ref,
                o_ref, xpad_ref, carry_ref, a_ref, b_ref):
    ts, c = x_ref.shape
    hb = c // LRU_HEADS

    @pl.when(pl.program_id(1) == 0)
    def _():
        xpad_ref[...] = jnp.zeros_like(xpad_ref)
        carry_ref[...] = jnp.zeros_like(carry_ref)

    xpad_ref[0:SUBLANE, :] = xpad_ref[ts:ts + SUBLANE, :]
    xpad_ref[SUBLANE:, :] = x_ref[...]
    u = cb_ref[...] + cw_ref[LRU_CONV - 1:LRU_CONV, :] * x_ref[...]
    for k in range(LRU_CONV - 1):
        off = SUBLANE - (LRU_CONV - 1) + k
        u = u + cw_ref[k:k + 1, :] * xpad_ref[off:off + ts, :]

    ub = u.astype(BF16)
    r_parts, i_parts = [], []
    for h in range(LRU_HEADS):
        uh = ub[:, h * hb:(h + 1) * hb]
        r_parts.append(jnp.dot(uh, wa_ref[h], preferred_element_type=F32))
        i_parts.append(jnp.dot(uh, wx_ref[h], preferred_element_type=F32))
    r = jax.nn.sigmoid(jnp.concatenate(r_parts, axis=1) + ba_ref[...])
    ig = jax.nn.sigmoid(jnp.concatenate(i_parts, axis=1) + bx_ref[...])
    nl = -lam_ref[...]
    softplus = jnp.maximum(nl, 0.0) + jnp.log1p(jnp.exp(-jnp.abs(nl)))
    log_a = (-LRU_C) * r * softplus
    a = jnp.exp(log_a)
    a_ref[...] = a
    z = -jnp.tanh(log_a) * (a * a + 1.0)
    root = jnp.where(z > 0.0, z * lax.rsqrt(z), 0.0)
    b_ref[...] = root * (ig * u)

    row = lax.broadcasted_iota(jnp.int32, (SUBLANE, c), 0)

    def group(gi, carry):
        r0 = pl.multiple_of(gi * SUBLANE, SUBLANE)
        a8 = a_ref[pl.ds(r0, SUBLANE), :]
        b8 = b_ref[pl.ds(r0, SUBLANE), :]
        for d in (1, 2, 4):
            a_sh = pltpu.roll(a8, d, 0)
            b_sh = pltpu.roll(b8, d, 0)
            m = row >= d
            b8 = jnp.where(m, a8 * b_sh + b8, b8)
            a8 = jnp.where(m, a8 * a_sh, a8)
        h8 = a8 * carry + b8
        b_ref[pl.ds(r0, SUBLANE), :] = h8
        return jnp.broadcast_to(h8[SUBLANE - 1:SUBLANE, :], (SUBLANE, c))

    carry_ref[...] = lax.fori_loop(0, ts // SUBLANE, group, carry_ref[...])

    y = b_ref[...] * jax.nn.gelu(gate_ref[...])
    o_ref[...] = _rms(y, g_ref[...]).astype(o_ref.dtype)


def _lru(proj, bsz, s, cw, cb, wa, ba, wx, bx, lam, g, ts):
    c = cw.shape[1]
    ns = s // ts
    row = lambda b, i: b * ns + i
    vec = pl.BlockSpec((1, c), lambda b, i: (0, 0))
    mat = pl.BlockSpec(wa.shape, lambda b, i: (0, 0, 0))
    return pl.pallas_call(
        _lru_kernel,
        grid=(bsz, ns),
        in_specs=[pl.BlockSpec((ts, c), lambda b, i: (row(b, i), 0)),
                  pl.BlockSpec((ts, c), lambda b, i: (row(b, i), 1)),
                  pl.BlockSpec((LRU_CONV, c), lambda b, i: (0, 0)), vec,
                  mat, vec, mat, vec, vec, vec],
        out_specs=pl.BlockSpec((ts, c), lambda b, i: (row(b, i), 0)),
        out_shape=jax.ShapeDtypeStruct((bsz * s, c), BF16),
        scratch_shapes=[pltpu.VMEM((ts + SUBLANE, c), F32),
                        pltpu.VMEM((SUBLANE, c), F32),
                        pltpu.VMEM((ts, c), F32),
                        pltpu.VMEM((ts, c), F32)],
        compiler_params=_cparams(("arbitrary", "arbitrary")),
        name="rg_lru",
    )(proj, proj, cw, cb, wa, ba, wx, bx, lam, g)


def _compress_one(kv_ref, pe_ref, w1_ref, b1_ref, w2_ref):
    s, dh = kv_ref.shape
    n = s // CMP_STRIDE
    half = CMP_STRIDE * dh
    xa = jnp.concatenate([kv_ref[pl.ds(l, n, stride=CMP_STRIDE), :] for l in range(CMP_STRIDE)],
                         axis=1).astype(BF16)
    lo = jnp.dot(xa, w1_ref[0:half, :], preferred_element_type=F32)
    hi = jnp.dot(xa, w1_ref[half:2 * half, :], preferred_element_type=F32)
    pe = jnp.broadcast_to(pe_ref[...], (SUBLANE, 2 * half)).astype(BF16)
    pe_term = jnp.dot(pe, w1_ref[...], preferred_element_type=F32)[0:1, :]
    pre = lo + pltpu.roll(hi, n - 1, 0) + pe_term + b1_ref[...]
    return jnp.dot(jax.nn.gelu(pre).astype(BF16), w2_ref[...], preferred_element_type=F32)


def _compress_kernel(kc_ref, vc_ref, pek_ref, w1k_ref, b1k_ref, w2k_ref,
                     pev_ref, w1v_ref, b1v_ref, w2v_ref, ko_ref, vo_ref):
    ko_ref[0] = _compress_one(kc_ref, pek_ref, w1k_ref, b1k_ref, w2k_ref)
    vo_ref[0] = _compress_one(vc_ref, pev_ref, w1v_ref, b1v_ref, w2v_ref)


def _compress(kvc, bsz, s, pk, pv):
    dh = HEAD_DIM
    kc_blk, vc_blk = 0, N_KV_HEADS
    n = s // CMP_STRIDE
    full = lambda a: pl.BlockSpec(a.shape, lambda b, h: (0,) * a.ndim)
    out = pl.BlockSpec((1, n, dh), lambda b, h: (b * N_KV_HEADS + h, 0, 0))
    shp = jax.ShapeDtypeStruct((bsz * N_KV_HEADS, n, dh), F32)
    return pl.pallas_call(
        _compress_kernel,
        grid=(bsz, N_KV_HEADS),
        in_specs=[pl.BlockSpec((s, dh), lambda b, h: (b, kc_blk + h)),
                  pl.BlockSpec((s, dh), lambda b, h: (b, vc_blk + h)),
                  *[full(a) for a in pk], *[full(a) for a in pv]],
        out_specs=[out, out],
        out_shape=[shp, shp],
        compiler_params=_cparams(("arbitrary", "arbitrary")),
        name="compress_kv",
    )(kvc, kvc, *pk, *pv)


KEY_CHUNK = 512
ROW_GROUPS = 2


def _attn_kernel(q_ref, ksel_ref, vsel_ref, kw_ref, vw_ref, gate_ref, kcmp_ref, vcmp_ref, ovt_ref, *rest, n_cast):
    cast_in, o_ref, cast_out = rest[:n_cast], rest[n_cast], rest[n_cast + 1:2 * n_cast + 1]
    (kaug_ref, vsaug_ref, kwb_ref, vwaug_ref, dbias_ref, wbias_ref,
     qaug_ref, s_a, s_b, p_a, p_b, acc_ref, m_ref, a_ref,
     qb_ref, sc_ref, pc_ref, oc_ref, sw_ref, ew_ref, ow_ref) = rest[2 * n_cast + 1:]
    for src, dst in zip(cast_in, cast_out):
        dst[...] = src[...].astype(dst.dtype)
    i = pl.program_id(2)
    s_len, dh = ksel_ref.shape
    tq = q_ref.shape[0]
    n_cmp = kcmp_ref.shape[1]
    n_sel = s_len // SEL_BLOCK
    gh = GQA // ROW_GROUPS
    rg = gh * tq

    @pl.when(i == 0)
    def _():
        kaug_ref[:, 0:dh] = ksel_ref[...].astype(BF16)
        kblk = lax.broadcasted_iota(jnp.int32, (s_len, LANE), 0) // SEL_BLOCK
        lane = lax.broadcasted_iota(jnp.int32, (s_len, LANE), 1)
        kaug_ref[:, dh:dh + LANE] = jnp.where(kblk == lane, 1.0, 0.0).astype(BF16)
        ones = jnp.ones((s_len, LANE), BF16)
        vsaug_ref[:, 0:dh] = vsel_ref[...].astype(BF16)
        vsaug_ref[:, dh:dh + LANE] = ones
        kwb_ref[...] = kw_ref[...].astype(BF16)
        vwaug_ref[:, 0:dh] = vw_ref[...].astype(BF16)
        vwaug_ref[:, dh:dh + LANE] = ones
        for v in range(dbias_ref.shape[0]):
            col = lax.broadcasted_iota(jnp.int32, (rg, KEY_CHUNK), 1)
            tok = lax.broadcasted_iota(jnp.int32, (rg, KEY_CHUNK), 0) & (tq - 1)
            dbias_ref[v] = jnp.where(col <= v * tq + tok, 0.0, NEG_INF)
        for v in range(wbias_ref.shape[0]):
            col = lax.broadcasted_iota(jnp.int32, wbias_ref.shape[1:], 1)
            tok = lax.broadcasted_iota(jnp.int32, wbias_ref.shape[1:], 0) & (tq - 1)
            d = v * tq + tok - col
            wbias_ref[v] = jnp.where(d >= 0, jnp.where(d < WINDOW, 0.0, NEG_INF), NEG_INF)

    nt = (((1,), (1,)), ((), ()))
    qs = i * tq
    qf = q_ref[...]
    qb = [jnp.concatenate([qf[:, g * dh:(g + 1) * dh] for g in range(sp * gh, (sp + 1) * gh)],
                          axis=0).astype(BF16) for sp in range(ROW_GROUPS)]
    t_row = qs + (lax.broadcasted_iota(jnp.int32, (rg, 1), 0) & (tq - 1))

    wlen = wbias_ref.shape[2]
    w0 = pl.multiple_of(jnp.maximum(qs + tq - wlen, 0), tq)
    band_bias = wbias_ref[jnp.minimum(i, wbias_ref.shape[0] - 1)]
    kcb = kcmp_ref[0].astype(BF16)
    vcb = vcmp_ref[0].astype(BF16)
    vis_c = (lax.broadcasted_iota(jnp.int32, (rg, n_cmp), 1) * CMP_STRIDE + (CMP_BLOCK - 1)) <= t_row

    def win_probs(sp):
        m_w = jnp.max(sw_ref[sp], axis=1, keepdims=True)
        return jnp.exp2(sw_ref[sp] - m_w).astype(BF16)

    for sp in range(ROW_GROUPS):
        qb_ref[sp] = qb[sp]
    for sp in range(ROW_GROUPS):
        sc_ref[sp] = jnp.where(vis_c, lax.dot_general(qb_ref[sp], kcb, nt, preferred_element_type=F32), NEG_INF)
        sw_ref[sp] = lax.dot_general(qb_ref[sp], kwb_ref[pl.ds(w0, wlen), :], nt,
                                     preferred_element_type=F32) + band_bias
    for sp in range(ROW_GROUPS):
        m_c = jnp.max(sc_ref[sp], axis=1, keepdims=True)
        e_c = jnp.where(vis_c, jnp.exp2(sc_ref[sp] - m_c), 0.0)
        pc_ref[sp] = e_c
        den_c = jnp.sum(e_c, axis=1, keepdims=True)
        pc_ref[sp] = pc_ref[sp] / jnp.where(den_c > 0.0, den_c, 1.0)
        oc_ref[sp] = jnp.dot(pc_ref[sp].astype(BF16), vcb, preferred_element_type=F32)
    p_sum = None
    for sp in range(ROW_GROUPS):
        for g in range(gh):
            pg = pc_ref[sp, g * tq:(g + 1) * tq, :]
            p_sum = pg if p_sum is None else p_sum + pg

    ov_b = ovt_ref[...].astype(BF16)
    imp_t, rest_p = None, p_sum
    for _ in range(3):
        piece = rest_p.astype(BF16)
        part = lax.dot_general(ov_b, piece, nt, preferred_element_type=F32)
        imp_t = part if imp_t is None else imp_t + part
        rest_p = rest_p - piece.astype(F32)
    blk = lax.broadcasted_iota(jnp.int32, (n_sel, tq), 0)
    cur = (qs + lax.broadcasted_iota(jnp.int32, (n_sel, tq), 1)) // SEL_BLOCK
    age = cur - blk
    score = jnp.where(blk == 0, FORCE_SCORE,
                      jnp.where(age < 0, -1.0, jnp.where(age < N_LOCAL_SEL, FORCE_SCORE, imp_t)))
    parts = [jnp.zeros((n_sel, tq), jnp.int32) for _ in range(4)]
    for j in range(n_sel):
        rj = score[j:j + 1, :]
        parts[j % 4] = parts[j % 4] + jnp.where(blk > j, jnp.where(rj >= score, 1, 0),
                                                jnp.where(rj > score, 1, 0))
    rank = (parts[0] + parts[1]) + (parts[2] + parts[3])
    bias_t = jnp.where(rank < min(N_SEL, n_sel), 0.0, NEG_INF)
    bias_t = jnp.concatenate([bias_t, jnp.zeros((LANE - n_sel, tq), F32)], axis=0)
    bias = jnp.concatenate([bias_t.T.astype(BF16)] * gh, axis=0)
    for sp in range(ROW_GROUPS):
        qaug_ref[sp] = jnp.concatenate([qb_ref[sp], bias], axis=1)

    def scores(sp, k0):
        return lax.dot_general(qaug_ref[sp], kaug_ref[pl.ds(k0, KEY_CHUNK), :], nt,
                               preferred_element_type=F32)

    def accumulate(sp, p, alpha, k0):
        pv = jnp.dot(p, vsaug_ref[pl.ds(k0, KEY_CHUNK), :], preferred_element_type=F32)
        acc_ref[sp] = alpha * acc_ref[sp] + pv

    def soften(sp, s_buf):
        m_old = m_ref[sp]
        m_new = jnp.maximum(m_old, jnp.max(s_buf[sp], axis=1, keepdims=True))
        m_ref[sp] = m_new
        return jnp.exp2(m_old - m_new), jnp.exp2(s_buf[sp] - m_new).astype(BF16)

    n_full = qs // KEY_CHUNK
    k_diag = pl.multiple_of(n_full * KEY_CHUNK, KEY_CHUNK)

    def step(c, s_in, p_prev, s_out, p_out):
        k_next = pl.multiple_of((c + 1) * KEY_CHUNK, KEY_CHUNK)
        k_prev = pl.multiple_of(jnp.where(c == 0, n_full, c - 1) * KEY_CHUNK, KEY_CHUNK)
        for sp in range(ROW_GROUPS):
            accumulate(sp, p_prev[sp], a_ref[sp], k_prev)
            if s_out is not None:
                s_out[sp] = scores(sp, k_next)
        for sp in range(ROW_GROUPS):
            a_ref[sp], p_out[sp] = soften(sp, s_in)

    causal_bias = dbias_ref[i % (KEY_CHUNK // tq)]
    for sp in range(ROW_GROUPS):
        m_ref[sp] = jnp.full((rg, 1), NEG_INF, F32)
        acc_ref[sp] = jnp.zeros((rg, dh + LANE), F32)
        s_b[sp] = scores(sp, k_diag) + causal_bias
        a_ref[sp], p_b[sp] = soften(sp, s_b)
        s_a[sp] = scores(sp, 0)

    for sp in range(ROW_GROUPS):
        ew_ref[sp] = win_probs(sp)
    for sp in range(ROW_GROUPS):
        pv = jnp.dot(ew_ref[sp], vwaug_ref[pl.ds(w0, wlen), :], preferred_element_type=F32)
        ow_ref[sp] = pv[:, :dh] / pv[:, dh:]
    gate = jax.nn.sigmoid(gate_ref[...])
    for g in range(GQA):
        sp, gg = divmod(g, gh)
        sl = slice(gg * tq, (gg + 1) * tq)
        o_ref[:, g * dh:(g + 1) * dh] = (gate[:, N_BRANCH * g:N_BRANCH * g + 1] * oc_ref[sp, sl, :]
                                         + gate[:, N_BRANCH * g + 2:N_BRANCH * g + 3] * ow_ref[sp, sl, :])

    def pair(k, carry):
        step(2 * k, s_a, p_b, s_b, p_a)
        step(2 * k + 1, s_b, p_a, s_a, p_b)
        return carry

    n_steps = jnp.maximum(n_full - 1, 0)
    lax.fori_loop(0, n_steps // 2, pair, 0)
    k_last = pl.multiple_of(n_steps * KEY_CHUNK, KEY_CHUNK)

    @pl.when(n_full == 0)
    def _():
        for sp in range(ROW_GROUPS):
            accumulate(sp, p_b[sp], a_ref[sp], k_diag)

    @pl.when((n_full > 0) & (n_steps % 2 == 0))
    def _():
        step(n_steps, s_a, p_b, None, p_a)
        for sp in range(ROW_GROUPS):
            accumulate(sp, p_a[sp], a_ref[sp], k_last)

    @pl.when((n_full > 0) & (n_steps % 2 == 1))
    def _():
        step(n_steps - 1, s_a, p_b, s_b, p_a)
        step(n_steps, s_b, p_a, None, p_b)
        for sp in range(ROW_GROUPS):
            accumulate(sp, p_b[sp], a_ref[sp], k_last)

    gate = jax.nn.sigmoid(gate_ref[...])
    for g in range(GQA):
        sp, gg = divmod(g, gh)
        acc = acc_ref[sp, gg * tq:(gg + 1) * tq, :]
        o_ref[:, g * dh:(g + 1) * dh] += gate[:, N_BRANCH * g + 1:N_BRANCH * g + 2] * (acc[:, :dh] / acc[:, dh:])


def _attention(q, kv4, gates, k_cmp, v_cmp, ov_t, bsz, s, cast=()):
    dh = HEAD_DIM
    tq = Q_BLOCK
    nq = s // tq
    rg = (GQA // ROW_GROUPS) * tq
    n_cmp = k_cmp.shape[1]
    wlen = min(WINDOW + tq, s)
    kv = lambda blk: pl.BlockSpec((s, dh), lambda b, h, i: (b, blk + h))
    cmp_spec = pl.BlockSpec((1, n_cmp, dh), lambda b, h, i: (b * N_KV_HEADS + h, 0, 0))
    cast_specs, cast_shapes = _cast_rider_specs(cast, bsz * N_KV_HEADS * nq,
                                                lambda b, h, i: ((b * N_KV_HEADS + h) * nq + i, 0))
    return pl.pallas_call(
        functools.partial(_attn_kernel, n_cast=len(cast)),
        grid=(bsz, N_KV_HEADS, nq),
        in_specs=[pl.BlockSpec((tq, GQA * dh), lambda b, h, i: (b * nq + i, h)),
                  kv(0), kv(N_KV_HEADS), kv(2 * N_KV_HEADS), kv(3 * N_KV_HEADS),
                  pl.BlockSpec((tq, LANE), lambda b, h, i: (b * nq + i, h)),
                  cmp_spec, cmp_spec,
                  pl.BlockSpec(ov_t.shape, lambda b, h, i: (0, 0)), *cast_specs],
        out_specs=[pl.BlockSpec((tq, GQA * dh), lambda b, h, i: (b * nq + i, h)), *cast_specs],
        out_shape=[jax.ShapeDtypeStruct((bsz * s, N_HEADS * dh), F32), *cast_shapes],
        scratch_shapes=[pltpu.VMEM((s, dh + LANE), BF16),
                        pltpu.VMEM((s, dh + LANE), BF16),
                        pltpu.VMEM((s, dh), BF16),
                        pltpu.VMEM((s, dh + LANE), BF16),
                        pltpu.VMEM((KEY_CHUNK // tq, rg, KEY_CHUNK), F32),
                        pltpu.VMEM((WINDOW // tq + 1, rg, wlen), F32),
                        pltpu.VMEM((ROW_GROUPS, rg, dh + LANE), BF16),
                        pltpu.VMEM((ROW_GROUPS, rg, KEY_CHUNK), F32),
                        pltpu.VMEM((ROW_GROUPS, rg, KEY_CHUNK), F32),
                        pltpu.VMEM((ROW_GROUPS, rg, KEY_CHUNK), BF16),
                        pltpu.VMEM((ROW_GROUPS, rg, KEY_CHUNK), BF16),
                        pltpu.VMEM((ROW_GROUPS, rg, dh + LANE), F32),
                        pltpu.VMEM((ROW_GROUPS, rg, 1), F32),
                        pltpu.VMEM((ROW_GROUPS, rg, 1), F32),
                        pltpu.VMEM((ROW_GROUPS, rg, dh), BF16),
                        pltpu.VMEM((ROW_GROUPS, rg, n_cmp), F32),
                        pltpu.VMEM((ROW_GROUPS, rg, n_cmp), F32),
                        pltpu.VMEM((ROW_GROUPS, rg, dh), F32),
                        pltpu.VMEM((ROW_GROUPS, rg, wlen), F32),
                        pltpu.VMEM((ROW_GROUPS, rg, wlen), BF16),
                        pltpu.VMEM((ROW_GROUPS, rg, dh), F32)],
        compiler_params=_cparams(("arbitrary", "arbitrary", "arbitrary")),
        name="nsa_attention",
    )(q, kv4, kv4, kv4, kv4, gates, k_cmp, v_cmp, ov_t, *cast)


WIN_CHUNK = 256


def _attn_t_kernel(q_ref, ksel_ref, vsel_ref, kw_ref, vw_ref, gate_ref, kcmp_ref, vcmp_ref, ovt_ref, *rest, n_cast):
    cast_in, o_ref, cast_out = rest[:n_cast], rest[n_cast], rest[n_cast + 1:2 * n_cast + 1]
    (kaug_ref, vst_ref, kwb_ref, vwt_ref, dbias_ref, wbias_ref,
     qt_ref, qaug_ref, s_a, s_b, p_a, p_b, acc_ref, m_ref, a_ref,
     sc_ref, pc_ref, oc_ref, sw_ref, ew_ref, ow_ref) = rest[2 * n_cast + 1:]
    for src, dst in zip(cast_in, cast_out):
        dst[...] = src[...].astype(dst.dtype)
    i = pl.program_id(2)
    s_len, dh = ksel_ref.shape
    tq = q_ref.shape[0]
    n_cmp = kcmp_ref.shape[1]
    n_sel = s_len // SEL_BLOCK
    gh = GQA // ROW_GROUPS
    rg = gh * tq
    wlen = wbias_ref.shape[1]
    n_wc = wlen // WIN_CHUNK

    def transposed(x):
        return x.astype(F32).T

    @pl.when(i == 0)
    def _():
        kaug_ref[:, 0:dh] = ksel_ref[...].astype(BF16)
        kblk = lax.broadcasted_iota(jnp.int32, (s_len, LANE), 0) // SEL_BLOCK
        lane = lax.broadcasted_iota(jnp.int32, (s_len, LANE), 1)
        kaug_ref[:, dh:dh + LANE] = jnp.where(kblk == lane, 1.0, 0.0).astype(BF16)
        kwb_ref[...] = kw_ref[...].astype(BF16)
        for c in range(s_len // KEY_CHUNK):
            vst_ref[c, 0:dh, :] = transposed(vsel_ref[c * KEY_CHUNK:(c + 1) * KEY_CHUNK, :]).astype(BF16)
            vst_ref[c, dh:dh + LANE, :] = jnp.ones((LANE, KEY_CHUNK), BF16)
        for c in range(s_len // WIN_CHUNK):
            vwt_ref[c, 0:dh, :] = transposed(vw_ref[c * WIN_CHUNK:(c + 1) * WIN_CHUNK, :]).astype(BF16)
            vwt_ref[c, dh:dh + LANE, :] = jnp.ones((LANE, WIN_CHUNK), BF16)
        tok_d = lax.broadcasted_iota(jnp.int32, (KEY_CHUNK, rg), 1) & (tq - 1)
        key_d = lax.broadcasted_iota(jnp.int32, (KEY_CHUNK, rg), 0)
        for v in range(dbias_ref.shape[0]):
            dbias_ref[v] = jnp.where(key_d <= v * tq + tok_d, 0.0, NEG_INF)
        tok_w = lax.broadcasted_iota(jnp.int32, (wlen, rg), 1) & (tq - 1)
        key_w = lax.broadcasted_iota(jnp.int32, (wlen, rg), 0)
        for v in range(wbias_ref.shape[0]):
            d = v * tq + tok_w - key_w
            wbias_ref[v] = jnp.where(d >= 0, jnp.where(d < WINDOW, 0.0, NEG_INF), NEG_INF)

    qs = i * tq
    t_lane = qs + (lax.broadcasted_iota(jnp.int32, (1, rg), 1) & (tq - 1))
    for sp in range(ROW_GROUPS):
        qt_ref[sp] = jnp.concatenate([transposed(q_ref[:, g * dh:(g + 1) * dh])
                                      for g in range(sp * gh, (sp + 1) * gh)], axis=1).astype(BF16)

    w0 = pl.multiple_of(jnp.maximum(qs + tq - wlen, 0), tq)
    band_bias = wbias_ref[jnp.minimum(i, wbias_ref.shape[0] - 1)]
    kcb = kcmp_ref[0].astype(BF16)
    vct = transposed(vcmp_ref[0]).astype(BF16)
    vis_c = (lax.broadcasted_iota(jnp.int32, (n_cmp, rg), 0) * CMP_STRIDE + (CMP_BLOCK - 1)) <= t_lane
    for sp in range(ROW_GROUPS):
        sc_ref[sp] = jnp.where(vis_c, jnp.dot(kcb, qt_ref[sp], preferred_element_type=F32), NEG_INF)
        sw_ref[sp] = jnp.dot(kwb_ref[pl.ds(w0, wlen), :], qt_ref[sp], preferred_element_type=F32) + band_bias
    for sp in range(ROW_GROUPS):
        m_c = jnp.max(sc_ref[sp], axis=0, keepdims=True)
        e_c = jnp.where(vis_c, jnp.exp2(sc_ref[sp] - m_c), 0.0)
        pc_ref[sp] = e_c
        den_c = jnp.sum(e_c, axis=0, keepdims=True)
        pc_ref[sp] = pc_ref[sp] / jnp.where(den_c > 0.0, den_c, 1.0)
        oc_ref[sp] = jnp.dot(vct, pc_ref[sp].astype(BF16), preferred_element_type=F32)
    p_sum = None
    for sp in range(ROW_GROUPS):
        for g in range(gh):
            pg = pc_ref[sp, :, g * tq:(g + 1) * tq]
            p_sum = pg if p_sum is None else p_sum + pg

    ov_b = ovt_ref[...].astype(BF16)
    imp_t, rest_p = None, p_sum
    for _ in range(3):
        piece = rest_p.astype(BF16)
        part = jnp.dot(ov_b, piece, preferred_element_type=F32)
        imp_t = part if imp_t is None else imp_t + part
        rest_p = rest_p - piece.astype(F32)
    blk = lax.broadcasted_iota(jnp.int32, (n_sel, tq), 0)
    cur = (qs + lax.broadcasted_iota(jnp.int32, (n_sel, tq), 1)) // SEL_BLOCK
    age = cur - blk
    score = jnp.where(blk == 0, FORCE_SCORE,
                      jnp.where(age < 0, -1.0, jnp.where(age < N_LOCAL_SEL, FORCE_SCORE, imp_t)))
    parts = [jnp.zeros((n_sel, tq), jnp.int32) for _ in range(4)]
    for j in range(n_sel):
        rj = score[j:j + 1, :]
        parts[j % 4] = parts[j % 4] + jnp.where(blk > j, jnp.where(rj >= score, 1, 0),
                                                jnp.where(rj > score, 1, 0))
    rank = (parts[0] + parts[1]) + (parts[2] + parts[3])
    bias_t = jnp.where(rank < min(N_SEL, n_sel), 0.0, NEG_INF)
    bias_t = jnp.concatenate([bias_t, jnp.zeros((LANE - n_sel, tq), F32)], axis=0).astype(BF16)
    bias = jnp.concatenate([bias_t] * gh, axis=1)
    for sp in range(ROW_GROUPS):
        qaug_ref[sp] = jnp.concatenate([qt_ref[sp], bias], axis=0)

    def scores(sp, k0):
        return jnp.dot(kaug_ref[pl.ds(k0, KEY_CHUNK), :], qaug_ref[sp], preferred_element_type=F32)

    def accumulate(sp, p, alpha, c):
        acc_ref[sp] = alpha * acc_ref[sp] + jnp.dot(vst_ref[c], p, preferred_element_type=F32)

    def soften(sp, s_buf):
        m_old = m_ref[sp]
        m_new = jnp.maximum(m_old, jnp.max(s_buf[sp], axis=0, keepdims=True))
        m_ref[sp] = m_new
        return jnp.exp2(m_old - m_new), jnp.exp2(s_buf[sp] - m_new).astype(BF16)

    n_full = qs // KEY_CHUNK

    def step(c, s_in, p_prev, s_out, p_out):
        k_next = pl.multiple_of((c + 1) * KEY_CHUNK, KEY_CHUNK)
        c_prev = jnp.where(c == 0, n_full, c - 1)
        for sp in range(ROW_GROUPS):
            accumulate(sp, p_prev[sp], a_ref[sp], c_prev)
            if s_out is not None:
                s_out[sp] = scores(sp, k_next)
        for sp in range(ROW_GROUPS):
            a_ref[sp], p_out[sp] = soften(sp, s_in)

    causal_bias = dbias_ref[i % (KEY_CHUNK // tq)]
    for sp in range(ROW_GROUPS):
        m_ref[sp] = jnp.full((1, rg), NEG_INF, F32)
        acc_ref[sp] = jnp.zeros((dh + LANE, rg), F32)
        s_b[sp] = scores(sp, pl.multiple_of(n_full * KEY_CHUNK, KEY_CHUNK)) + causal_bias
        a_ref[sp], p_b[sp] = soften(sp, s_b)
        s_a[sp] = scores(sp, 0)

    wc0 = w0 // WIN_CHUNK
    for sp in range(ROW_GROUPS):
        m_w = jnp.max(sw_ref[sp], axis=0, keepdims=True)
        ew_ref[sp] = jnp.exp2(sw_ref[sp] - m_w).astype(BF16)
    for sp in range(ROW_GROUPS):
        pv = None
        for jw in range(n_wc):
            part = jnp.dot(vwt_ref[wc0 + jw], ew_ref[sp, jw * WIN_CHUNK:(jw + 1) * WIN_CHUNK, :],
                           preferred_element_type=F32)
            pv = part if pv is None else pv + part
        ow_ref[sp] = pv[:dh] / pv[dh:]

    def pair(k, carry):
        step(2 * k, s_a, p_b, s_b, p_a)
        step(2 * k + 1, s_b, p_a, s_a, p_b)
        return carry

    n_steps = jnp.maximum(n_full - 1, 0)
    lax.fori_loop(0, n_steps // 2, pair, 0)

    @pl.when(n_full == 0)
    def _():
        for sp in range(ROW_GROUPS):
            accumulate(sp, p_b[sp], a_ref[sp], n_full)

    @pl.when((n_full > 0) & (n_steps % 2 == 0))
    def _():
        step(n_steps, s_a, p_b, None, p_a)
        for sp in range(ROW_GROUPS):
            accumulate(sp, p_a[sp], a_ref[sp], n_steps)

    @pl.when((n_full > 0) & (n_steps % 2 == 1))
    def _():
        step(n_steps - 1, s_a, p_b, s_b, p_a)
        step(n_steps, s_b, p_a, None, p_b)
        for sp in range(ROW_GROUPS):
            accumulate(sp, p_b[sp], a_ref[sp], n_steps)

    gate_t = transposed(jax.nn.sigmoid(gate_ref[...]))
    for g in range(GQA):
        sp, gg = divmod(g, gh)
        ln = slice(gg * tq, (gg + 1) * tq)
        o_s = acc_ref[sp, 0:dh, ln] / acc_ref[sp, dh:dh + LANE, ln]
        tot = (gate_t[N_BRANCH * g:N_BRANCH * g + 1, :] * oc_ref[sp, :, ln]
               + gate_t[N_BRANCH * g + 1:N_BRANCH * g + 2, :] * o_s
               + gate_t[N_BRANCH * g + 2:N_BRANCH * g + 3, :] * ow_ref[sp, :, ln])
        o_ref[:, g * dh:(g + 1) * dh] = tot.T


def _attention_t(q, kv4, gates, k_cmp, v_cmp, ov_t, bsz, s, cast=()):
    dh = HEAD_DIM
    tq = Q_BLOCK
    nq = s // tq
    rg = (GQA // ROW_GROUPS) * tq
    n_cmp = k_cmp.shape[1]
    wlen = min(WINDOW + tq, s)
    kv = lambda blk: pl.BlockSpec((s, dh), lambda b, h, i: (b, blk + h))
    cmp_spec = pl.BlockSpec((1, n_cmp, dh), lambda b, h, i: (b * N_KV_HEADS + h, 0, 0))
    cast_specs, cast_shapes = _cast_rider_specs(cast, bsz * N_KV_HEADS * nq,
                                                lambda b, h, i: ((b * N_KV_HEADS + h) * nq + i, 0))
    grp = lambda rows, dt: pltpu.VMEM((ROW_GROUPS, rows, rg), dt)
    return pl.pallas_call(
        functools.partial(_attn_t_kernel, n_cast=len(cast)),
        grid=(bsz, N_KV_HEADS, nq),
        in_specs=[pl.BlockSpec((tq, GQA * dh), lambda b, h, i: (b * nq + i, h)),
                  kv(0), kv(N_KV_HEADS), kv(2 * N_KV_HEADS), kv(3 * N_KV_HEADS),
                  pl.BlockSpec((tq, LANE), lambda b, h, i: (b * nq + i, h)),
                  cmp_spec, cmp_spec,
                  pl.BlockSpec(ov_t.shape, lambda b, h, i: (0, 0)), *cast_specs],
        out_specs=[pl.BlockSpec((tq, GQA * dh), lambda b, h, i: (b * nq + i, h)), *cast_specs],
        out_shape=[jax.ShapeDtypeStruct((bsz * s, N_HEADS * dh), F32), *cast_shapes],
        scratch_shapes=[pltpu.VMEM((s, dh + LANE), BF16),
                        pltpu.VMEM((s // KEY_CHUNK, dh + LANE, KEY_CHUNK), BF16),
                        pltpu.VMEM((s, dh), BF16),
                        pltpu.VMEM((s // WIN_CHUNK, dh + LANE, WIN_CHUNK), BF16),
                        pltpu.VMEM((KEY_CHUNK // tq, KEY_CHUNK, rg), F32),
                        pltpu.VMEM((WINDOW // tq + 1, wlen, rg), F32),
                        grp(dh, BF16), grp(dh + LANE, BF16),
                        grp(KEY_CHUNK, F32), grp(KEY_CHUNK, F32),
                        grp(KEY_CHUNK, BF16), grp(KEY_CHUNK, BF16),
                        grp(dh + LANE, F32), grp(1, F32), grp(1, F32),
                        grp(n_cmp, F32), grp(n_cmp, F32), grp(dh, F32),
                        grp(wlen, F32), grp(wlen, BF16), grp(dh, F32)],
        compiler_params=_cparams(("arbitrary", "arbitrary", "arbitrary")),
        name="nsa_attention",
    )(q, kv4, kv4, kv4, kv4, gates, k_cmp, v_cmp, ov_t, *cast)


def _outproj_kernel(lru_ref, attn_ref, ga_ref, x_ref, w_ref, gf_ref, h_ref, xn_ref):
    c = lru_ref.shape[1]
    an = _rms(attn_ref[...], ga_ref[...]).astype(BF16)
    h = (x_ref[...]
         + jnp.dot(lru_ref[...], w_ref[0:c, :], preferred_element_type=F32)
         + jnp.dot(an, w_ref[c:, :], preferred_element_type=F32))
    h_ref[...] = h
    xn_ref[...] = _rms(h, gf_ref[...]).astype(BF16)


def _out_proj(lru_n, attn, g_attn, x2, w_out, g_ffn, tm):
    t, d = x2.shape
    c = lru_n.shape[1]
    return pl.pallas_call(
        _outproj_kernel,
        grid=(t // tm,),
        in_specs=[pl.BlockSpec((tm, c), lambda i: (i, 0)),
                  pl.BlockSpec((tm, attn.shape[1]), lambda i: (i, 0)),
                  pl.BlockSpec((1, attn.shape[1]), lambda i: (0, 0)),
                  pl.BlockSpec((tm, d), lambda i: (i, 0)),
                  pl.BlockSpec(w_out.shape, lambda i: (0, 0), pipeline_mode=pl.Buffered(1)),
                  pl.BlockSpec((1, d), lambda i: (0, 0))],
        out_specs=[pl.BlockSpec((tm, d), lambda i: (i, 0)),
                   pl.BlockSpec((tm, d), lambda i: (i, 0))],
        out_shape=[jax.ShapeDtypeStruct((t, d), F32), jax.ShapeDtypeStruct((t, d), BF16)],
        compiler_params=_cparams(("arbitrary",)),
        name="out_proj",
    )(lru_n, attn, g_attn, x2, w_out, g_ffn)


def _ffn_kernel(xn_ref, wu_ref, wv_ref, cw_ref, cb_ref, wd_ref, h_ref, g_ref, o_ref,
                halo_ref, upad_ref, *, tiles_per_seq, final_norm):
    i = pl.program_id(0)
    j = pl.program_id(1)
    tm = xn_ref.shape[0]

    @pl.when(j == 0)
    def _():
        o_ref[...] = h_ref[...]

    xn = xn_ref[...]
    u = jnp.dot(xn, wu_ref[...], preferred_element_type=F32)
    v = jnp.dot(xn, wv_ref[...], preferred_element_type=F32)
    first = (i % tiles_per_seq) == 0
    upad_ref[0:SUBLANE, :] = jnp.where(first, 0.0, halo_ref[j])
    upad_ref[SUBLANE:, :] = u
    halo_ref[j] = u[tm - SUBLANE:, :]
    uc = cb_ref[...] + cw_ref[FFN_CONV - 1:FFN_CONV, :] * u
    for k in range(FFN_CONV - 1):
        off = SUBLANE - (FFN_CONV - 1) + k
        uc = uc + cw_ref[k:k + 1, :] * upad_ref[off:off + tm, :]
    act = (jax.nn.gelu(uc) * v).astype(BF16)
    o_ref[...] += jnp.dot(act, wd_ref[...], preferred_element_type=F32)

    if final_norm:
        @pl.when(j == pl.num_programs(1) - 1)
        def _():
            o_ref[...] = _rms(o_ref[...], g_ref[...])


def _ffn(xn, h1, w_up, cw, cb, w_down, g_final, final_norm, s, tm, tf):
    t, d = xn.shape
    dff = w_down.shape[0]
    nj = dff // tf
    return pl.pallas_call(
        functools.partial(_ffn_kernel, tiles_per_seq=s // tm, final_norm=final_norm),
        grid=(t // tm, nj),
        in_specs=[pl.BlockSpec((tm, d), lambda i, j: (i, 0)),
                  pl.BlockSpec((d, tf), lambda i, j: (0, j)),
                  pl.BlockSpec((d, tf), lambda i, j: (0, nj + j)),
                  pl.BlockSpec((FFN_CONV, tf), lambda i, j: (0, j)),
                  pl.BlockSpec((1, tf), lambda i, j: (0, j)),
                  pl.BlockSpec((tf, d), lambda i, j: (j, 0)),
                  pl.BlockSpec((tm, d), lambda i, j: (i, 0)),
                  pl.BlockSpec((1, d), lambda i, j: (0, 0))],
        out_specs=pl.BlockSpec((tm, d), lambda i, j: (i, 0)),
        out_shape=jax.ShapeDtypeStruct((t, d), F32),
        scratch_shapes=[pltpu.VMEM((nj, SUBLANE, tf), F32),
                        pltpu.VMEM((tm + SUBLANE, tf), F32)],
        compiler_params=_cparams(("arbitrary", "arbitrary")),
        name="conv_ffn",
    )(xn, w_up, w_up, cw, cb, w_down, h1, g_final)


def _overlap_t(n_cmp, n_sel):
    c0 = jnp.arange(n_cmp)[None, :] * CMP_STRIDE
    s0 = jnp.arange(n_sel)[:, None] * SEL_BLOCK
    ov = jnp.clip(jnp.minimum(c0 + CMP_BLOCK, s0 + SEL_BLOCK) - jnp.maximum(c0, s0), 0)
    return ov.astype(F32) / CMP_BLOCK


def kernel(x, g_mix, w_in, lru_conv_w, lru_conv_b, lru_wa, lru_ba, lru_wx, lru_bx, lru_lambda, cmp_pe_k, cmp_w1_k, cmp_b1_k, cmp_w2_k, cmp_pe_v, cmp_w1_v, cmp_b1_v, cmp_w2_v, g_lru_out, g_attn_out, w_out, g_ffn, w_up, ffn_conv_w, ffn_conv_b, w_down, g_final):
    bsz, s, d = x.shape
    depth = w_in.shape[0]
    c_lru = lru_conv_w.shape[2]
    attn_w = N_HEADS * HEAD_DIM
    kv_w = N_KV_HEADS * HEAD_DIM
    n_gate = N_BRANCH * GQA
    main_w = 2 * c_lru + attn_w + 6 * kv_w
    segments = ((0, 2 * c_lru, F32, None),
                (2 * c_lru, attn_w, BF16, HEAD_DIM ** -0.5 * LOG2_E),
                (2 * c_lru + attn_w, 2 * kv_w, F32, None),
                (2 * c_lru + attn_w + 2 * kv_w, 4 * kv_w, BF16, None))
    ov_t = _overlap_t(s // CMP_STRIDE, s // SEL_BLOCK)
    row = lambda a: a.reshape(1, -1)

    h = x.reshape(bsz * s, d)
    for l in range(depth):
        w_main = w_in[l].astype(BF16)
        gpad = jnp.zeros((d, LANE - n_gate), F32)
        w_gate = jnp.concatenate([p for hh in range(N_KV_HEADS)
                                  for p in (w_in[l][:, main_w + hh * n_gate:main_w + (hh + 1) * n_gate], gpad)],
                                 axis=1).astype(BF16)
        lru_in, q, kvc, kv4, gates = _in_proj(h, row(g_mix[l]), w_main, segments, w_gate, tm=ROW_TILE)
        lru_n = _lru(lru_in, bsz, s, lru_conv_w[l], row(lru_conv_b[l]), lru_wa[l].astype(BF16),
                        row(lru_ba[l]), lru_wx[l].astype(BF16), row(lru_bx[l]), row(lru_lambda[l]),
                        row(g_lru_out[l]), ts=LRU_TILE)
        pk = (cmp_pe_k[l].reshape(1, -1), cmp_w1_k[l].astype(BF16), row(cmp_b1_k[l]), cmp_w2_k[l].astype(BF16))
        pv = (cmp_pe_v[l].reshape(1, -1), cmp_w1_v[l].astype(BF16), row(cmp_b1_v[l]), cmp_w2_v[l].astype(BF16))
        k_cmp, v_cmp = _compress(kvc, bsz, s, pk, pv)
        attn, w_out_b, w_down_b, w_up_b = _attention_t(q, kv4, gates, k_cmp, v_cmp, ov_t, bsz, s,
                                                     cast=(w_out[l], w_down[l], w_up[l]))
        h1, xn = _out_proj(lru_n, attn, row(g_attn_out[l]), h, w_out_b, row(g_ffn[l]), tm=ROW_TILE)
        h = _ffn(xn, h1, w_up_b, ffn_conv_w[l], row(ffn_conv_b[l]), w_down_b,
                 row(g_final), l == depth - 1, s, tm=ROW_TILE, tf=FF_TILE)
    return h.reshape(bsz, s, d)
```

```python
import functools

import jax
import jax.numpy as jnp
from jax import lax
from jax.experimental import pallas as pl
from jax.experimental.pallas import tpu as pltpu

F32 = jnp.float32
BF16 = jnp.bfloat16

LANE = 128
SUBLANE = 8
VMEM_LIMIT = 56 * 1024 * 1024

LRU_HEADS = 8
LRU_CONV = 4
LRU_C = 8.0
N_HEADS = 8
N_KV_HEADS = 2
GQA = N_HEADS // N_KV_HEADS
HEAD_DIM = 128
N_BRANCH = 3
CMP_BLOCK = 32
CMP_STRIDE = 16
SEL_BLOCK = 64
N_SEL = 16
N_LOCAL_SEL = 2
WINDOW = 512
Q_BLOCK = 256
FFN_CONV = 3
EPS = 1e-6
NEG_INF = -1e30
FORCE_SCORE = 1e4
LOG2_E = 1.4426950408889634

ROW_TILE = 512
FF_TILE = 1024
LRU_TILE = 256


def _rms(x, g):
    return x * lax.rsqrt(jnp.mean(x * x, axis=-1, keepdims=True) + EPS) * g


def _cparams(sem):
    return pltpu.CompilerParams(dimension_semantics=sem, vmem_limit_bytes=VMEM_LIMIT)


def _inproj_kernel(x_ref, g_ref, w_ref, wg_ref, *o_refs, segments):
    xn = _rms(x_ref[...], g_ref[...]).astype(BF16)
    for (c0, width, scale), o_ref in zip(segments, o_refs):
        y = jnp.dot(xn, w_ref[:, c0:c0 + width], preferred_element_type=F32)
        o_ref[...] = (y if scale is None else y * scale).astype(o_ref.dtype)
    o_refs[-1][...] = jnp.dot(xn, wg_ref[...], preferred_element_type=F32)


def _in_proj(x2, g, w_all, segments, w_gate, tm):
    t, d = x2.shape
    ng = w_gate.shape[1]
    n = max(c0 + width for c0, width, _, _ in segments)
    resident = lambda shape: pl.BlockSpec(shape, lambda i: (0, 0), pipeline_mode=pl.Buffered(1))
    return pl.pallas_call(
        functools.partial(_inproj_kernel, segments=tuple((c0, width, sc) for c0, width, _, sc in segments)),
        grid=(t // tm,),
        in_specs=[pl.BlockSpec((tm, d), lambda i: (i, 0)),
                  pl.BlockSpec((1, d), lambda i: (0, 0)),
                  resident((d, n)), resident(w_gate.shape)],
        out_specs=[*[pl.BlockSpec((tm, width), lambda i: (i, 0)) for _, width, _, _ in segments],
                   pl.BlockSpec((tm, ng), lambda i: (i, 0))],
        out_shape=[*[jax.ShapeDtypeStruct((t, width), dt) for _, width, dt, _ in segments],
                   jax.ShapeDtypeStruct((t, ng), F32)],
        compiler_params=_cparams(("arbitrary",)),
        name="in_proj",
    )(x2, g, w_all, w_gate)


def _cast_rider_specs(arrays, n_steps, index):
    specs, shapes = [], []
    for a in arrays:
        rows = a.shape[0] // n_steps
        assert rows * n_steps == a.shape[0] and rows % (2 * SUBLANE) == 0
        specs.append(pl.BlockSpec((rows, a.shape[1]), index))
        shapes.append(jax.ShapeDtypeStruct(a.shape, BF16))
    return specs, shapes


def _lru_kernel(x_ref, gate_ref, cw_ref, cb_ref, wa_ref, ba_ref, wx_ref, bx_ref, lam_ref, g_ref,
                o_ref, xpad_ref, carry_ref, a_ref, b_ref):
    ts, c = x_ref.shape
    hb = c // LRU_HEADS

    @pl.when(pl.program_id(1) == 0)
    def _():
        xpad_ref[...] = jnp.zeros_like(xpad_ref)
        carry_ref[...] = jnp.zeros_like(carry_ref)

    xpad_ref[0:SUBLANE, :] = xpad_ref[ts:ts + SUBLANE, :]
    xpad_ref[SUBLANE:, :] = x_ref[...]
    u = cb_ref[...] + cw_ref[LRU_CONV - 1:LRU_CONV, :] * x_ref[...]
    for k in range(LRU_CONV - 1):
        off = SUBLANE - (LRU_CONV - 1) + k
        u = u + cw_ref[k:k + 1, :] * xpad_ref[off:off + ts, :]

    ub = u.astype(BF16)
    r_parts, i_parts = [], []
    for h in range(LRU_HEADS):
        uh = ub[:, h * hb:(h + 1) * hb]
        r_parts.append(jnp.dot(uh, wa_ref[h], preferred_element_type=F32))
        i_parts.append(jnp.dot(uh, wx_ref[h], preferred_element_type=F32))
    r = jax.nn.sigmoid(jnp.concatenate(r_parts, axis=1) + ba_ref[...])
    ig = jax.nn.sigmoid(jnp.concatenate(i_parts, axis=1) + bx_ref[...])
    nl = -lam_ref[...]
    softplus = jnp.maximum(nl, 0.0) + jnp.log1p(jnp.exp(-jnp.abs(nl)))
    log_a = (-LRU_C) * r * softplus
    a = jnp.exp(log_a)
    a_ref[...] = a
    z = -jnp.tanh(log_a) * (a * a + 1.0)
    root = jnp.where(z > 0.0, z * lax.rsqrt(z), 0.0)
    b_ref[...] = root * (ig * u)

    row = lax.broadcasted_iota(jnp.int32, (SUBLANE, c), 0)

    def group(gi, carry):
        r0 = pl.multiple_of(gi * SUBLANE, SUBLANE)
        a8 = a_ref[pl.ds(r0, SUBLANE), :]
        b8 = b_ref[pl.ds(r0, SUBLANE), :]
        for d in (1, 2, 4):
            a_sh = pltpu.roll(a8, d, 0)
            b_sh = pltpu.roll(b8, d, 0)
            m = row >= d
            b8 = jnp.where(m, a8 * b_sh + b8, b8)
            a8 = jnp.where(m, a8 * a_sh, a8)
        h8 = a8 * carry + b8
        b_ref[pl.ds(r0, SUBLANE), :] = h8
        return jnp.broadcast_to(h8[SUBLANE - 1:SUBLANE, :], (SUBLANE, c))

    carry_ref[...] = lax.fori_loop(0, ts // SUBLANE, group, carry_ref[...])

    y = b_ref[...] * jax.nn.gelu(gate_ref[...])
    o_ref[...] = _rms(y, g_ref[...]).astype(o_ref.dtype)


def _lru(proj, bsz, s, cw, cb, wa, ba, wx, bx, lam, g, ts):
    c = cw.shape[1]
    ns = s // ts
    row = lambda b, i: b * ns + i
    vec = pl.BlockSpec((1, c), lambda b, i: (0, 0))
    mat = pl.BlockSpec(wa.shape, lambda b, i: (0, 0, 0))
    return pl.pallas_call(
        _lru_kernel,
        grid=(bsz, ns),
        in_specs=[pl.BlockSpec((ts, c), lambda b, i: (row(b, i), 0)),
                  pl.BlockSpec((ts, c), lambda b, i: (row(b, i), 1)),
                  pl.BlockSpec((LRU_CONV, c), lambda b, i: (0, 0)), vec,
                  mat, vec, mat, vec, vec, vec],
        out_specs=pl.BlockSpec((ts, c), lambda b, i: (row(b, i), 0)),
        out_shape=jax.ShapeDtypeStruct((bsz * s, c), BF16),
        scratch_shapes=[pltpu.VMEM((ts + SUBLANE, c), F32),
                        pltpu.VMEM((SUBLANE, c), F32),
                        pltpu.VMEM((ts, c), F32),
                        pltpu.VMEM((ts, c), F32)],
        compiler_params=_cparams(("arbitrary", "arbitrary")),
        name="rg_lru",
    )(proj, proj, cw, cb, wa, ba, wx, bx, lam, g)


def _compress_one(kv_ref, pe_ref, w1_ref, b1_ref, w2_ref):
    s, dh = kv_ref.shape
    n = s // CMP_STRIDE
    half = CMP_STRIDE * dh
    xa = jnp.concatenate([kv_ref[pl.ds(l, n, stride=CMP_STRIDE), :] for l in range(CMP_STRIDE)],
                         axis=1).astype(BF16)
    lo = jnp.dot(xa, w1_ref[0:half, :], preferred_element_type=F32)
    hi = jnp.dot(xa, w1_ref[half:2 * half, :], preferred_element_type=F32)
    pe = jnp.broadcast_to(pe_ref[...], (SUBLANE, 2 * half)).astype(BF16)
    pe_term = jnp.dot(pe, w1_ref[...], preferred_element_type=F32)[0:1, :]
    pre = lo + pltpu.roll(hi, n - 1, 0) + pe_term + b1_ref[...]
    return jnp.dot(jax.nn.gelu(pre).astype(BF16), w2_ref[...], preferred_element_type=F32)


def _compress_kernel(kc_ref, vc_ref, pek_ref, w1k_ref, b1k_ref, w2k_ref,
                     pev_ref, w1v_ref, b1v_ref, w2v_ref, ko_ref, vo_ref):
    ko_ref[0] = _compress_one(kc_ref, pek_ref, w1k_ref, b1k_ref, w2k_ref)
    vo_ref[0] = _compress_one(vc_ref, pev_ref, w1v_ref, b1v_ref, w2v_ref)


def _compress(kvc, bsz, s, pk, pv):
    dh = HEAD_DIM
    kc_blk, vc_blk = 0, N_KV_HEADS
    n = s // CMP_STRIDE
    full = lambda a: pl.BlockSpec(a.shape, lambda b, h: (0,) * a.ndim)
    out = pl.BlockSpec((1, n, dh), lambda b, h: (b * N_KV_HEADS + h, 0, 0))
    shp = jax.ShapeDtypeStruct((bsz * N_KV_HEADS, n, dh), F32)
    return pl.pallas_call(
        _compress_kernel,
        grid=(bsz, N_KV_HEADS),
        in_specs=[pl.BlockSpec((s, dh), lambda b, h: (b, kc_blk + h)),
                  pl.BlockSpec((s, dh), lambda b, h: (b, vc_blk + h)),
                  *[full(a) for a in pk], *[full(a) for a in pv]],
        out_specs=[out, out],
        out_shape=[shp, shp],
        compiler_params=_cparams(("arbitrary", "arbitrary")),
        name="compress_kv",
    )(kvc, kvc, *pk, *pv)


KEY_CHUNK = 512
ROW_GROUPS = 2
WIN_CHUNK = 256


def _attn_t_kernel(q_ref, ksel_ref, vsel_ref, kw_ref, vw_ref, gate_ref, kcmp_ref, vcmp_ref, ovt_ref, *rest, n_cast):
    cast_in, o_ref, cast_out = rest[:n_cast], rest[n_cast], rest[n_cast + 1:2 * n_cast + 1]
    (kaug_ref, vst_ref, kwb_ref, vwt_ref, dbias_ref, wbias_ref,
     qt_ref, qaug_ref, s_a, s_b, p_a, p_b, acc_ref, m_ref, a_ref,
     sc_ref, pc_ref, oc_ref, sw_ref, ew_ref, ow_ref) = rest[2 * n_cast + 1:]
    for src, dst in zip(cast_in, cast_out):
        dst[...] = src[...].astype(dst.dtype)
    i = pl.program_id(2)
    s_len, dh = ksel_ref.shape
    tq = q_ref.shape[0]
    n_cmp = kcmp_ref.shape[1]
    n_sel = s_len // SEL_BLOCK
    gh = GQA // ROW_GROUPS
    rg = gh * tq
    wlen = wbias_ref.shape[1]
    n_wc = wlen // WIN_CHUNK

    def transposed(x):
        return x.astype(F32).T

    @pl.when(i == 0)
    def _():
        kaug_ref[:, 0:dh] = ksel_ref[...].astype(BF16)
        kblk = lax.broadcasted_iota(jnp.int32, (s_len, LANE), 0) // SEL_BLOCK
        lane = lax.broadcasted_iota(jnp.int32, (s_len, LANE), 1)
        kaug_ref[:, dh:dh + LANE] = jnp.where(kblk == lane, 1.0, 0.0).astype(BF16)
        kwb_ref[...] = kw_ref[...].astype(BF16)
        for c in range(s_len // KEY_CHUNK):
            vst_ref[c, 0:dh, :] = transposed(vsel_ref[c * KEY_CHUNK:(c + 1) * KEY_CHUNK, :]).astype(BF16)
            vst_ref[c, dh:dh + LANE, :] = jnp.ones((LANE, KEY_CHUNK), BF16)
        for c in range(s_len // WIN_CHUNK):
            vwt_ref[c, 0:dh, :] = transposed(vw_ref[c * WIN_CHUNK:(c + 1) * WIN_CHUNK, :]).astype(BF16)
            vwt_ref[c, dh:dh + LANE, :] = jnp.ones((LANE, WIN_CHUNK), BF16)
        tok_d = lax.broadcasted_iota(jnp.int32, (KEY_CHUNK, tq), 1)
        key_d = lax.broadcasted_iota(jnp.int32, (KEY_CHUNK, tq), 0)
        for v in range(dbias_ref.shape[0]):
            dbias_ref[v] = jnp.where(key_d <= v * tq + tok_d, 0.0, NEG_INF)
        tok_w = lax.broadcasted_iota(jnp.int32, (wlen, tq), 1)
        key_w = lax.broadcasted_iota(jnp.int32, (wlen, tq), 0)
        for v in range(wbias_ref.shape[0]):
            d = v * tq + tok_w - key_w
            wbias_ref[v] = jnp.where(d >= 0, jnp.where(d < WINDOW, 0.0, NEG_INF), NEG_INF)

    qs = i * tq
    t_lane = qs + (lax.broadcasted_iota(jnp.int32, (1, rg), 1) & (tq - 1))
    for sp in range(ROW_GROUPS):
        qt_ref[sp] = jnp.concatenate([transposed(q_ref[:, g * dh:(g + 1) * dh])
                                      for g in range(sp * gh, (sp + 1) * gh)], axis=1).astype(BF16)

    w0 = pl.multiple_of(jnp.maximum(qs + tq - wlen, 0), tq)
    band_bias = jnp.concatenate([wbias_ref[jnp.minimum(i, wbias_ref.shape[0] - 1)]] * gh, axis=1)
    kcb = kcmp_ref[0].astype(BF16)
    vct = transposed(vcmp_ref[0]).astype(BF16)
    vis_c = (lax.broadcasted_iota(jnp.int32, (n_cmp, rg), 0) * CMP_STRIDE + (CMP_BLOCK - 1)) <= t_lane
    for sp in range(ROW_GROUPS):
        sc_ref[sp] = jnp.where(vis_c, jnp.dot(kcb, qt_ref[sp], preferred_element_type=F32), NEG_INF)
        sw_ref[sp] = jnp.dot(kwb_ref[pl.ds(w0, wlen), :], qt_ref[sp], preferred_element_type=F32) + band_bias
    for sp in range(ROW_GROUPS):
        m_c = jnp.max(sc_ref[sp], axis=0, keepdims=True)
        e_c = jnp.where(vis_c, jnp.exp2(sc_ref[sp] - m_c), 0.0)
        pc_ref[sp] = e_c
        den_c = jnp.sum(e_c, axis=0, keepdims=True)
        pc_ref[sp] = pc_ref[sp] / jnp.where(den_c > 0.0, den_c, 1.0)
        oc_ref[sp] = jnp.dot(vct, pc_ref[sp].astype(BF16), preferred_element_type=F32)
    p_sum = None
    for sp in range(ROW_GROUPS):
        for g in range(gh):
            pg = pc_ref[sp, :, g * tq:(g + 1) * tq]
            p_sum = pg if p_sum is None else p_sum + pg

    ov_b = ovt_ref[...].astype(BF16)
    imp_t, rest_p = None, p_sum
    for _ in range(3):
        piece = rest_p.astype(BF16)
        part = jnp.dot(ov_b, piece, preferred_element_type=F32)
        imp_t = part if imp_t is None else imp_t + part
        rest_p = rest_p - piece.astype(F32)
    blk = lax.broadcasted_iota(jnp.int32, (n_sel, tq), 0)
    cur = (qs + lax.broadcasted_iota(jnp.int32, (n_sel, tq), 1)) // SEL_BLOCK
    age = cur - blk
    score = jnp.where(blk == 0, FORCE_SCORE,
                      jnp.where(age < 0, -1.0, jnp.where(age < N_LOCAL_SEL, FORCE_SCORE, imp_t)))
    parts = [jnp.zeros((n_sel, tq), jnp.int32) for _ in range(4)]
    for j in range(n_sel):
        rj = score[j:j + 1, :]
        parts[j % 4] = parts[j % 4] + jnp.where(blk > j, jnp.where(rj >= score, 1, 0),
                                                jnp.where(rj > score, 1, 0))
    rank = (parts[0] + parts[1]) + (parts[2] + parts[3])
    bias_t = jnp.where(rank < min(N_SEL, n_sel), 0.0, NEG_INF)
    bias_t = jnp.concatenate([bias_t, jnp.zeros((LANE - n_sel, tq), F32)], axis=0).astype(BF16)
    bias = jnp.concatenate([bias_t] * gh, axis=1)
    for sp in range(ROW_GROUPS):
        qaug_ref[sp] = jnp.concatenate([qt_ref[sp], bias], axis=0)

    def scores(sp, k0):
        return jnp.dot(kaug_ref[pl.ds(k0, KEY_CHUNK), :], qaug_ref[sp], preferred_element_type=F32)

    def accumulate(sp, p, alpha, c):
        acc_ref[sp] = alpha * acc_ref[sp] + jnp.dot(vst_ref[c], p, preferred_element_type=F32)

    def soften(sp, s_buf):
        m_old = m_ref[sp]
        m_new = jnp.maximum(m_old, jnp.max(s_buf[sp], axis=0, keepdims=True))
        m_ref[sp] = m_new
        return jnp.exp2(m_old - m_new), jnp.exp2(s_buf[sp] - m_new).astype(BF16)

    n_full = qs // KEY_CHUNK

    def step(c, s_in, p_prev, s_out, p_out):
        k_next = pl.multiple_of((c + 1) * KEY_CHUNK, KEY_CHUNK)
        c_prev = jnp.where(c == 0, n_full, c - 1)
        for sp in range(ROW_GROUPS):
            accumulate(sp, p_prev[sp], a_ref[sp], c_prev)
            if s_out is not None:
                s_out[sp] = scores(sp, k_next)
        for sp in range(ROW_GROUPS):
            a_ref[sp], p_out[sp] = soften(sp, s_in)

    causal_bias = jnp.concatenate([dbias_ref[i % (KEY_CHUNK // tq)]] * gh, axis=1)
    for sp in range(ROW_GROUPS):
        m_ref[sp] = jnp.full((1, rg), NEG_INF, F32)
        acc_ref[sp] = jnp.zeros((dh + LANE, rg), F32)
        s_b[sp] = scores(sp, pl.multiple_of(n_full * KEY_CHUNK, KEY_CHUNK)) + causal_bias
        a_ref[sp], p_b[sp] = soften(sp, s_b)
        s_a[sp] = scores(sp, 0)

    wc0 = w0 // WIN_CHUNK
    for sp in range(ROW_GROUPS):
        m_w = jnp.max(sw_ref[sp], axis=0, keepdims=True)
        ew_ref[sp] = jnp.exp2(sw_ref[sp] - m_w).astype(BF16)
    for sp in range(ROW_GROUPS):
        pv = None
        for jw in range(n_wc):
            part = jnp.dot(vwt_ref[wc0 + jw], ew_ref[sp, jw * WIN_CHUNK:(jw + 1) * WIN_CHUNK, :],
                           preferred_element_type=F32)
            pv = part if pv is None else pv + part
        ow_ref[sp] = pv[:dh] / pv[dh:]

    def pair(k, carry):
        step(2 * k, s_a, p_b, s_b, p_a)
        step(2 * k + 1, s_b, p_a, s_a, p_b)
        return carry

    n_steps = jnp.maximum(n_full - 1, 0)
    lax.fori_loop(0, n_steps // 2, pair, 0)

    @pl.when(n_full == 0)
    def _():
        for sp in range(ROW_GROUPS):
            accumulate(sp, p_b[sp], a_ref[sp], n_full)

    @pl.when((n_full > 0) & (n_steps % 2 == 0))
    def _():
        step(n_steps, s_a, p_b, None, p_a)
        for sp in range(ROW_GROUPS):
            accumulate(sp, p_a[sp], a_ref[sp], n_steps)

    @pl.when((n_full > 0) & (n_steps % 2 == 1))
    def _():
        step(n_steps - 1, s_a, p_b, s_b, p_a)
        step(n_steps, s_b, p_a, None, p_b)
        for sp in range(ROW_GROUPS):
            accumulate(sp, p_b[sp], a_ref[sp], n_steps)

    gate_t = transposed(jax.nn.sigmoid(gate_ref[...]))
    for g in range(GQA):
        sp, gg = divmod(g, gh)
        ln = slice(gg * tq, (gg + 1) * tq)
        o_s = acc_ref[sp, 0:dh, ln] / acc_ref[sp, dh:dh + LANE, ln]
        tot = (gate_t[N_BRANCH * g:N_BRANCH * g + 1, :] * oc_ref[sp, :, ln]
               + gate_t[N_BRANCH * g + 1:N_BRANCH * g + 2, :] * o_s
               + gate_t[N_BRANCH * g + 2:N_BRANCH * g + 3, :] * ow_ref[sp, :, ln])
        o_ref[:, g * dh:(g + 1) * dh] = tot.T


def _attention_t(q, kv4, gates, k_cmp, v_cmp, ov_t, bsz, s, cast=()):
    dh = HEAD_DIM
    tq = Q_BLOCK
    nq = s // tq
    rg = (GQA // ROW_GROUPS) * tq
    n_cmp = k_cmp.shape[1]
    wlen = min(WINDOW + tq, s)
    kv = lambda blk: pl.BlockSpec((s, dh), lambda b, h, i: (b, blk + h))
    cmp_spec = pl.BlockSpec((1, n_cmp, dh), lambda b, h, i: (b * N_KV_HEADS + h, 0, 0))
    cast_specs, cast_shapes = _cast_rider_specs(cast, bsz * N_KV_HEADS * nq,
                                                lambda b, h, i: ((b * N_KV_HEADS + h) * nq + i, 0))
    grp = lambda rows, dt: pltpu.VMEM((ROW_GROUPS, rows, rg), dt)
    return pl.pallas_call(
        functools.partial(_attn_t_kernel, n_cast=len(cast)),
        grid=(bsz, N_KV_HEADS, nq),
        in_specs=[pl.BlockSpec((tq, GQA * dh), lambda b, h, i: (b * nq + i, h)),
                  kv(0), kv(N_KV_HEADS), kv(2 * N_KV_HEADS), kv(3 * N_KV_HEADS),
                  pl.BlockSpec((tq, LANE), lambda b, h, i: (b * nq + i, h)),
                  cmp_spec, cmp_spec,
                  pl.BlockSpec(ov_t.shape, lambda b, h, i: (0, 0)), *cast_specs],
        out_specs=[pl.BlockSpec((tq, GQA * dh), lambda b, h, i: (b * nq + i, h)), *cast_specs],
        out_shape=[jax.ShapeDtypeStruct((bsz * s, N_HEADS * dh), F32), *cast_shapes],
        scratch_shapes=[pltpu.VMEM((s, dh + LANE), BF16),
                        pltpu.VMEM((s // KEY_CHUNK, dh + LANE, KEY_CHUNK), BF16),
                        pltpu.VMEM((s, dh), BF16),
                        pltpu.VMEM((s // WIN_CHUNK, dh + LANE, WIN_CHUNK), BF16),
                        pltpu.VMEM((KEY_CHUNK // tq, KEY_CHUNK, tq), F32),
                        pltpu.VMEM((WINDOW // tq + 1, wlen, tq), F32),
                        grp(dh, BF16), grp(dh + LANE, BF16),
                        grp(KEY_CHUNK, F32), grp(KEY_CHUNK, F32),
                        grp(KEY_CHUNK, BF16), grp(KEY_CHUNK, BF16),
                        grp(dh + LANE, F32), grp(1, F32), grp(1, F32),
                        grp(n_cmp, F32), grp(n_cmp, F32), grp(dh, F32),
                        grp(wlen, F32), grp(wlen, BF16), grp(dh, F32)],
        compiler_params=_cparams(("arbitrary", "arbitrary", "arbitrary")),
        name="nsa_attention",
    )(q, kv4, kv4, kv4, kv4, gates, k_cmp, v_cmp, ov_t, *cast)


def _outproj_kernel(lru_ref, attn_ref, ga_ref, x_ref, w_ref, gf_ref, h_ref, xn_ref):
    c = lru_ref.shape[1]
    an = _rms(attn_ref[...], ga_ref[...]).astype(BF16)
    h = (x_ref[...]
         + jnp.dot(lru_ref[...], w_ref[0:c, :], preferred_element_type=F32)
         + jnp.dot(an, w_ref[c:, :], preferred_element_type=F32))
    h_ref[...] = h
    xn_ref[...] = _rms(h, gf_ref[...]).astype(BF16)


def _out_proj(lru_n, attn, g_attn, x2, w_out, g_ffn, tm):
    t, d = x2.shape
    c = lru_n.shape[1]
    return pl.pallas_call(
        _outproj_kernel,
        grid=(t // tm,),
        in_specs=[pl.BlockSpec((tm, c), lambda i: (i, 0)),
                  pl.BlockSpec((tm, attn.shape[1]), lambda i: (i, 0)),
                  pl.BlockSpec((1, attn.shape[1]), lambda i: (0, 0)),
                  pl.BlockSpec((tm, d), lambda i: (i, 0)),
                  pl.BlockSpec(w_out.shape, lambda i: (0, 0), pipeline_mode=pl.Buffered(1)),
                  pl.BlockSpec((1, d), lambda i: (0, 0))],
        out_specs=[pl.BlockSpec((tm, d), lambda i: (i, 0)),
                   pl.BlockSpec((tm, d), lambda i: (i, 0))],
        out_shape=[jax.ShapeDtypeStruct((t, d), F32), jax.ShapeDtypeStruct((t, d), BF16)],
        compiler_params=_cparams(("arbitrary",)),
        name="out_proj",
    )(lru_n, attn, g_attn, x2, w_out, g_ffn)


def _ffn_kernel(xn_ref, wu_ref, wv_ref, cw_ref, cb_ref, wd_ref, h_ref, g_ref, o_ref,
                halo_ref, upad_ref, *, tiles_per_seq, final_norm):
    i = pl.program_id(0)
    j = pl.program_id(1)
    tm = xn_ref.shape[0]

    @pl.when(j == 0)
    def _():
        o_ref[...] = h_ref[...]

    xn = xn_ref[...]
    u = jnp.dot(xn, wu_ref[...], preferred_element_type=F32)
    v = jnp.dot(xn, wv_ref[...], preferred_element_type=F32)
    first = (i % tiles_per_seq) == 0
    upad_ref[0:SUBLANE, :] = jnp.where(first, 0.0, halo_ref[j])
    upad_ref[SUBLANE:, :] = u
    halo_ref[j] = u[tm - SUBLANE:, :]
    uc = cb_ref[...] + cw_ref[FFN_CONV - 1:FFN_CONV, :] * u
    for k in range(FFN_CONV - 1):
        off = SUBLANE - (FFN_CONV - 1) + k
        uc = uc + cw_ref[k:k + 1, :] * upad_ref[off:off + tm, :]
    act = (jax.nn.gelu(uc) * v).astype(BF16)
    o_ref[...] += jnp.dot(act, wd_ref[...], preferred_element_type=F32)

    if final_norm:
        @pl.when(j == pl.num_programs(1) - 1)
        def _():
            o_ref[...] = _rms(o_ref[...], g_ref[...])


def _ffn(xn, h1, w_up, cw, cb, w_down, g_final, final_norm, s, tm, tf):
    t, d = xn.shape
    dff = w_down.shape[0]
    nj = dff // tf
    return pl.pallas_call(
        functools.partial(_ffn_kernel, tiles_per_seq=s // tm, final_norm=final_norm),
        grid=(t // tm, nj),
        in_specs=[pl.BlockSpec((tm, d), lambda i, j: (i, 0)),
                  pl.BlockSpec((d, tf), lambda i, j: (0, j)),
                  pl.BlockSpec((d, tf), lambda i, j: (0, nj + j)),
                  pl.BlockSpec((FFN_CONV, tf), lambda i, j: (0, j)),
                  pl.BlockSpec((1, tf), lambda i, j: (0, j)),
                  pl.BlockSpec((tf, d), lambda i, j: (j, 0)),
                  pl.BlockSpec((tm, d), lambda i, j: (i, 0)),
                  pl.BlockSpec((1, d), lambda i, j: (0, 0))],
        out_specs=pl.BlockSpec((tm, d), lambda i, j: (i, 0)),
        out_shape=jax.ShapeDtypeStruct((t, d), F32),
        scratch_shapes=[pltpu.VMEM((nj, SUBLANE, tf), F32),
                        pltpu.VMEM((tm + SUBLANE, tf), F32)],
        compiler_params=_cparams(("arbitrary", "arbitrary")),
        name="conv_ffn",
    )(xn, w_up, w_up, cw, cb, w_down, h1, g_final)


def _overlap_t(n_cmp, n_sel):
    c0 = jnp.arange(n_cmp)[None, :] * CMP_STRIDE
    s0 = jnp.arange(n_sel)[:, None] * SEL_BLOCK
    ov = jnp.clip(jnp.minimum(c0 + CMP_BLOCK, s0 + SEL_BLOCK) - jnp.maximum(c0, s0), 0)
    return ov.astype(F32) / CMP_BLOCK


def kernel(x, g_mix, w_in, lru_conv_w, lru_conv_b, lru_wa, lru_ba, lru_wx, lru_bx, lru_lambda, cmp_pe_k, cmp_w1_k, cmp_b1_k, cmp_w2_k, cmp_pe_v, cmp_w1_v, cmp_b1_v, cmp_w2_v, g_lru_out, g_attn_out, w_out, g_ffn, w_up, ffn_conv_w, ffn_conv_b, w_down, g_final):
    bsz, s, d = x.shape
    depth = w_in.shape[0]
    c_lru = lru_conv_w.shape[2]
    attn_w = N_HEADS * HEAD_DIM
    kv_w = N_KV_HEADS * HEAD_DIM
    n_gate = N_BRANCH * GQA
    main_w = 2 * c_lru + attn_w + 6 * kv_w
    segments = ((0, 2 * c_lru, F32, None),
                (2 * c_lru, attn_w, BF16, HEAD_DIM ** -0.5 * LOG2_E),
                (2 * c_lru + attn_w, 2 * kv_w, F32, None),
                (2 * c_lru + attn_w + 2 * kv_w, 4 * kv_w, BF16, None))
    ov_t = _overlap_t(s // CMP_STRIDE, s // SEL_BLOCK)
    row = lambda a: a.reshape(1, -1)

    h = x.reshape(bsz * s, d)
    for l in range(depth):
        w_main = w_in[l].astype(BF16)
        gpad = jnp.zeros((d, LANE - n_gate), F32)
        w_gate = jnp.concatenate([p for hh in range(N_KV_HEADS)
                                  for p in (w_in[l][:, main_w + hh * n_gate:main_w + (hh + 1) * n_gate], gpad)],
                                 axis=1).astype(BF16)
        lru_in, q, kvc, kv4, gates = _in_proj(h, row(g_mix[l]), w_main, segments, w_gate, tm=ROW_TILE)
        lru_n = _lru(lru_in, bsz, s, lru_conv_w[l], row(lru_conv_b[l]), lru_wa[l].astype(BF16),
                        row(lru_ba[l]), lru_wx[l].astype(BF16), row(lru_bx[l]), row(lru_lambda[l]),
                        row(g_lru_out[l]), ts=LRU_TILE)
        pk = (cmp_pe_k[l].reshape(1, -1), cmp_w1_k[l].astype(BF16), row(cmp_b1_k[l]), cmp_w2_k[l].astype(BF16))
        pv = (cmp_pe_v[l].reshape(1, -1), cmp_w1_v[l].astype(BF16), row(cmp_b1_v[l]), cmp_w2_v[l].astype(BF16))
        k_cmp, v_cmp = _compress(kvc, bsz, s, pk, pv)
        attn, w_out_b, w_down_b, w_up_b = _attention_t(q, kv4, gates, k_cmp, v_cmp, ov_t, bsz, s,
                                                     cast=(w_out[l], w_down[l], w_up[l]))
        h1, xn = _out_proj(lru_n, attn, row(g_attn_out[l]), h, w_out_b, row(g_ffn[l]), tm=ROW_TILE)
        h = _ffn(xn, h1, w_up_b, ffn_conv_w[l], row(ffn_conv_b[l]), w_down_b,
                 row(g_final), l == depth - 1, s, tm=ROW_TILE, tf=FF_TILE)
    return h.reshape(bsz, s, d)
```

```python
import functools

import jax
import jax.numpy as jnp
from jax import lax
from jax.experimental import pallas as pl
from jax.experimental.pallas import tpu as pltpu

F32 = jnp.float32
BF16 = jnp.bfloat16

LANE = 128
SUBLANE = 8
VMEM_LIMIT = 56 * 1024 * 1024

LRU_HEADS = 8
LRU_CONV = 4
LRU_C = 8.0
N_HEADS = 8
N_KV_HEADS = 2
GQA = N_HEADS // N_KV_HEADS
HEAD_DIM = 128
N_BRANCH = 3
CMP_BLOCK = 32
CMP_STRIDE = 16
SEL_BLOCK = 64
N_SEL = 16
N_LOCAL_SEL = 2
WINDOW = 512
Q_BLOCK = 256
FFN_CONV = 3
EPS = 1e-6
NEG_INF = -1e30
FORCE_SCORE = 1e4
LOG2_E = 1.4426950408889634

ROW_TILE = 512
FF_TILE = 1024
LRU_TILE = 256


def _rms(x, g):
    return x * lax.rsqrt(jnp.mean(x * x, axis=-1, keepdims=True) + EPS) * g


def _cparams(sem):
    return pltpu.CompilerParams(dimension_semantics=sem, vmem_limit_bytes=VMEM_LIMIT)


def _inproj_kernel(x_ref, g_ref, w_ref, wg_ref, *o_refs, segments):
    xn = _rms(x_ref[...], g_ref[...]).astype(BF16)
    for (c0, width, scale), o_ref in zip(segments, o_refs):
        y = jnp.dot(xn, w_ref[:, c0:c0 + width], preferred_element_type=F32)
        o_ref[...] = (y if scale is None else y * scale).astype(o_ref.dtype)
    o_refs[-1][...] = jnp.dot(xn, wg_ref[...], preferred_element_type=F32)


def _in_proj(x2, g, w_all, segments, w_gate, tm):
    t, d = x2.shape
    ng = w_gate.shape[1]
    n = max(c0 + width for c0, width, _, _ in segments)
    resident = lambda shape: pl.BlockSpec(shape, lambda i: (0, 0), pipeline_mode=pl.Buffered(1))
    return pl.pallas_call(
        functools.partial(_inproj_kernel, segments=tuple((c0, width, sc) for c0, width, _, sc in segments)),
        grid=(t // tm,),
        in_specs=[pl.BlockSpec((tm, d), lambda i: (i, 0)),
                  pl.BlockSpec((1, d), lambda i: (0, 0)),
                  resident((d, n)), resident(w_gate.shape)],
        out_specs=[*[pl.BlockSpec((tm, width), lambda i: (i, 0)) for _, width, _, _ in segments],
                   pl.BlockSpec((tm, ng), lambda i: (i, 0))],
        out_shape=[*[jax.ShapeDtypeStruct((t, width), dt) for _, width, dt, _ in segments],
                   jax.ShapeDtypeStruct((t, ng), F32)],
        compiler_params=_cparams(("arbitrary",)),
        name="in_proj",
    )(x2, g, w_all, w_gate)


def _cast_rider_specs(arrays, n_steps, index):
    specs, shapes = [], []
    for a in arrays:
        rows = a.shape[0] // n_steps
        assert rows * n_steps == a.shape[0] and rows % (2 * SUBLANE) == 0
        specs.append(pl.BlockSpec((rows, a.shape[1]), index))
        shapes.append(jax.ShapeDtypeStruct(a.shape, BF16))
    return specs, shapes


def _lru_kernel(x_ref, gate_ref, cw_ref, cb_ref, wa_ref, ba_ref, wx_ref, bx_ref, lam_ref, g_ref,
                o_ref, xpad_ref, carry_ref, a_ref, b_ref):
    ts, c = x_ref.shape
    hb = c // LRU_HEADS

    @pl.when(pl.program_id(1) == 0)
    def _():
        xpad_ref[...] = jnp.zeros_like(xpad_ref)
        carry_ref[...] = jnp.zeros_like(carry_ref)

    xpad_ref[0:SUBLANE, :] = xpad_ref[ts:ts + SUBLANE, :]
    xpad_ref[SUBLANE:, :] = x_ref[...]
    u = cb_ref[...] + cw_ref[LRU_CONV - 1:LRU_CONV, :] * x_ref[...]
    for k in range(LRU_CONV - 1):
        off = SUBLANE - (LRU_CONV - 1) + k
        u = u + cw_ref[k:k + 1, :] * xpad_ref[off:off + ts, :]

    ub = u.astype(BF16)
    r_parts, i_parts = [], []
    for h in range(LRU_HEADS):
        uh = ub[:, h * hb:(h + 1) * hb]
        r_parts.append(jnp.dot(uh, wa_ref[h], preferred_element_type=F32))
        i_parts.append(jnp.dot(uh, wx_ref[h], preferred_element_type=F32))
    r = jax.nn.sigmoid(jnp.concatenate(r_parts, axis=1) + ba_ref[...])
    ig = jax.nn.sigmoid(jnp.concatenate(i_parts, axis=1) + bx_ref[...])
    nl = -lam_ref[...]
    softplus = jnp.maximum(nl, 0.0) + jnp.log1p(jnp.exp(-jnp.abs(nl)))
    log_a = (-LRU_C) * r * softplus
    a = jnp.exp(log_a)
    a_ref[...] = a
    z = -jnp.tanh(log_a) * (a * a + 1.0)
    root = jnp.where(z > 0.0, z * lax.rsqrt(z), 0.0)
    b_ref[...] = root * (ig * u)

    row = lax.broadcasted_iota(jnp.int32, (SUBLANE, c), 0)

    def group(gi, carry):
        r0 = pl.multiple_of(gi * SUBLANE, SUBLANE)
        a8 = a_ref[pl.ds(r0, SUBLANE), :]
        b8 = b_ref[pl.ds(r0, SUBLANE), :]
        for d in (1, 2, 4):
            a_sh = pltpu.roll(a8, d, 0)
            b_sh = pltpu.roll(b8, d, 0)
            m = row >= d
            b8 = jnp.where(m, a8 * b_sh + b8, b8)
            a8 = jnp.where(m, a8 * a_sh, a8)
        h8 = a8 * carry + b8
        b_ref[pl.ds(r0, SUBLANE), :] = h8
        return jnp.broadcast_to(h8[SUBLANE - 1:SUBLANE, :], (SUBLANE, c))

    carry_ref[...] = lax.fori_loop(0, ts // SUBLANE, group, carry_ref[...])

    y = b_ref[...] * jax.nn.gelu(gate_ref[...])
    o_ref[...] = _rms(y, g_ref[...]).astype(o_ref.dtype)


def _lru(proj, bsz, s, cw, cb, wa, ba, wx, bx, lam, g, ts):
    c = cw.shape[1]
    ns = s // ts
    row = lambda b, i: b * ns + i
    vec = pl.BlockSpec((1, c), lambda b, i: (0, 0))
    mat = pl.BlockSpec(wa.shape, lambda b, i: (0, 0, 0))
    return pl.pallas_call(
        _lru_kernel,
        grid=(bsz, ns),
        in_specs=[pl.BlockSpec((ts, c), lambda b, i: (row(b, i), 0)),
                  pl.BlockSpec((ts, c), lambda b, i: (row(b, i), 1)),
                  pl.BlockSpec((LRU_CONV, c), lambda b, i: (0, 0)), vec,
                  mat, vec, mat, vec, vec, vec],
        out_specs=pl.BlockSpec((ts, c), lambda b, i: (row(b, i), 0)),
        out_shape=jax.ShapeDtypeStruct((bsz * s, c), BF16),
        scratch_shapes=[pltpu.VMEM((ts + SUBLANE, c), F32),
                        pltpu.VMEM((SUBLANE, c), F32),
                        pltpu.VMEM((ts, c), F32),
                        pltpu.VMEM((ts, c), F32)],
        compiler_params=_cparams(("arbitrary", "arbitrary")),
        name="rg_lru",
    )(proj, proj, cw, cb, wa, ba, wx, bx, lam, g)


def _compress_one(kv_ref, pe_ref, w1_ref, b1_ref, w2_ref):
    s, dh = kv_ref.shape
    n = s // CMP_STRIDE
    half = CMP_STRIDE * dh
    xa = jnp.concatenate([kv_ref[pl.ds(l, n, stride=CMP_STRIDE), :] for l in range(CMP_STRIDE)],
                         axis=1).astype(BF16)
    lo = jnp.dot(xa, w1_ref[0:half, :], preferred_element_type=F32)
    hi = jnp.dot(xa, w1_ref[half:2 * half, :], preferred_element_type=F32)
    pe = jnp.broadcast_to(pe_ref[...], (SUBLANE, 2 * half)).astype(BF16)
    pe_term = jnp.dot(pe, w1_ref[...], preferred_element_type=F32)[0:1, :]
    pre = lo + pltpu.roll(hi, n - 1, 0) + pe_term + b1_ref[...]
    return jnp.dot(jax.nn.gelu(pre).astype(BF16), w2_ref[...], preferred_element_type=F32)


def _compress_kernel(kc_ref, vc_ref, pek_ref, w1k_ref, b1k_ref, w2k_ref,
                     pev_ref, w1v_ref, b1v_ref, w2v_ref, ko_ref, vo_ref):
    ko_ref[0] = _compress_one(kc_ref, pek_ref, w1k_ref, b1k_ref, w2k_ref)
    vo_ref[0] = _compress_one(vc_ref, pev_ref, w1v_ref, b1v_ref, w2v_ref)


def _compress(kvc, bsz, s, pk, pv):
    dh = HEAD_DIM
    kc_blk, vc_blk = 0, N_KV_HEADS
    n = s // CMP_STRIDE
    full = lambda a: pl.BlockSpec(a.shape, lambda b, h: (0,) * a.ndim)
    out = pl.BlockSpec((1, n, dh), lambda b, h: (b * N_KV_HEADS + h, 0, 0))
    shp = jax.ShapeDtypeStruct((bsz * N_KV_HEADS, n, dh), F32)
    return pl.pallas_call(
        _compress_kernel,
        grid=(bsz, N_KV_HEADS),
        in_specs=[pl.BlockSpec((s, dh), lambda b, h: (b, kc_blk + h)),
                  pl.BlockSpec((s, dh), lambda b, h: (b, vc_blk + h)),
                  *[full(a) for a in pk], *[full(a) for a in pv]],
        out_specs=[out, out],
        out_shape=[shp, shp],
        compiler_params=_cparams(("arbitrary", "arbitrary")),
        name="compress_kv",
    )(kvc, kvc, *pk, *pv)


KEY_CHUNK = 512
ROW_GROUPS = 2
ONES_ROWS = 16
WIN_CHUNK = 256


def _attn_t_kernel(q_ref, ksel_ref, vsel_ref, kw_ref, vw_ref, gate_ref, kcmp_ref, vcmp_ref, ovt_ref, *rest, n_cast):
    cast_in, o_ref, cast_out = rest[:n_cast], rest[n_cast], rest[n_cast + 1:2 * n_cast + 1]
    (kaug_ref, vst_ref, kwb_ref, vwt_ref, dbias_ref, wbias_ref,
     qt_ref, qaug_ref, s_a, s_b, p_a, p_b, acc_ref, m_ref, a_ref,
     sc_ref, pc_ref, oc_ref, sw_ref, ew_ref, ow_ref) = rest[2 * n_cast + 1:]
    for src, dst in zip(cast_in, cast_out):
        dst[...] = src[...].astype(dst.dtype)
    i = pl.program_id(2)
    s_len, dh = ksel_ref.shape
    tq = q_ref.shape[0]
    n_cmp = kcmp_ref.shape[1]
    n_sel = s_len // SEL_BLOCK
    gh = GQA // ROW_GROUPS
    rg = gh * tq
    wlen = wbias_ref.shape[1]
    n_wc = wlen // WIN_CHUNK

    def transposed(x):
        return x.astype(F32).T

    @pl.when(i == 0)
    def _():
        kaug_ref[:, 0:dh] = ksel_ref[...].astype(BF16)
        kblk = lax.broadcasted_iota(jnp.int32, (s_len, LANE), 0) // SEL_BLOCK
        lane = lax.broadcasted_iota(jnp.int32, (s_len, LANE), 1)
        kaug_ref[:, dh:dh + LANE] = jnp.where(kblk == lane, 1.0, 0.0).astype(BF16)
        kwb_ref[...] = kw_ref[...].astype(BF16)
        for c in range(s_len // KEY_CHUNK):
            vst_ref[c, 0:dh, :] = transposed(vsel_ref[c * KEY_CHUNK:(c + 1) * KEY_CHUNK, :]).astype(BF16)
            vst_ref[c, dh:dh + ONES_ROWS, :] = jnp.ones((ONES_ROWS, KEY_CHUNK), BF16)
        for c in range(s_len // WIN_CHUNK):
            vwt_ref[c, 0:dh, :] = transposed(vw_ref[c * WIN_CHUNK:(c + 1) * WIN_CHUNK, :]).astype(BF16)
            vwt_ref[c, dh:dh + ONES_ROWS, :] = jnp.ones((ONES_ROWS, WIN_CHUNK), BF16)
        tok_d = lax.broadcasted_iota(jnp.int32, (KEY_CHUNK, tq), 1)
        key_d = lax.broadcasted_iota(jnp.int32, (KEY_CHUNK, tq), 0)
        for v in range(dbias_ref.shape[0]):
            dbias_ref[v] = jnp.where(key_d <= v * tq + tok_d, 0.0, NEG_INF)
        tok_w = lax.broadcasted_iota(jnp.int32, (wlen, tq), 1)
        key_w = lax.broadcasted_iota(jnp.int32, (wlen, tq), 0)
        for v in range(wbias_ref.shape[0]):
            d = v * tq + tok_w - key_w
            wbias_ref[v] = jnp.where(d >= 0, jnp.where(d < WINDOW, 0.0, NEG_INF), NEG_INF)

    qs = i * tq
    t_lane = qs + (lax.broadcasted_iota(jnp.int32, (1, rg), 1) & (tq - 1))
    for sp in range(ROW_GROUPS):
        qt_ref[sp] = jnp.concatenate([transposed(q_ref[:, g * dh:(g + 1) * dh])
                                      for g in range(sp * gh, (sp + 1) * gh)], axis=1).astype(BF16)

    w0 = pl.multiple_of(jnp.maximum(qs + tq - wlen, 0), tq)
    band_bias = jnp.concatenate([wbias_ref[jnp.minimum(i, wbias_ref.shape[0] - 1)]] * gh, axis=1)
    kcb = kcmp_ref[0].astype(BF16)
    vct = transposed(vcmp_ref[0]).astype(BF16)
    vis_c = (lax.broadcasted_iota(jnp.int32, (n_cmp, rg), 0) * CMP_STRIDE + (CMP_BLOCK - 1)) <= t_lane
    for sp in range(ROW_GROUPS):
        sc_ref[sp] = jnp.where(vis_c, jnp.dot(kcb, qt_ref[sp], preferred_element_type=F32), NEG_INF)
        sw_ref[sp] = jnp.dot(kwb_ref[pl.ds(w0, wlen), :], qt_ref[sp], preferred_element_type=F32) + band_bias
    for sp in range(ROW_GROUPS):
        m_c = jnp.max(sc_ref[sp], axis=0, keepdims=True)
        e_c = jnp.where(vis_c, jnp.exp2(sc_ref[sp] - m_c), 0.0)
        pc_ref[sp] = e_c
        den_c = jnp.sum(e_c, axis=0, keepdims=True)
        pc_ref[sp] = pc_ref[sp] / jnp.where(den_c > 0.0, den_c, 1.0)
        oc_ref[sp] = jnp.dot(vct, pc_ref[sp].astype(BF16), preferred_element_type=F32)
    p_sum = None
    for sp in range(ROW_GROUPS):
        for g in range(gh):
            pg = pc_ref[sp, :, g * tq:(g + 1) * tq]
            p_sum = pg if p_sum is None else p_sum + pg

    ov_b = ovt_ref[...].astype(BF16)
    imp_t, rest_p = None, p_sum
    for _ in range(3):
        piece = rest_p.astype(BF16)
        part = jnp.dot(ov_b, piece, preferred_element_type=F32)
        imp_t = part if imp_t is None else imp_t + part
        rest_p = rest_p - piece.astype(F32)
    blk = lax.broadcasted_iota(jnp.int32, (n_sel, tq), 0)
    cur = (qs + lax.broadcasted_iota(jnp.int32, (n_sel, tq), 1)) // SEL_BLOCK
    age = cur - blk
    score = jnp.where(blk == 0, FORCE_SCORE,
                      jnp.where(age < 0, -1.0, jnp.where(age < N_LOCAL_SEL, FORCE_SCORE, imp_t)))
    parts = [jnp.zeros((n_sel, tq), jnp.int32) for _ in range(4)]
    for j in range(n_sel):
        rj = score[j:j + 1, :]
        parts[j % 4] = parts[j % 4] + jnp.where(blk > j, jnp.where(rj >= score, 1, 0),
                                                jnp.where(rj > score, 1, 0))
    rank = (parts[0] + parts[1]) + (parts[2] + parts[3])
    bias_t = jnp.where(rank < min(N_SEL, n_sel), 0.0, NEG_INF)
    bias_t = jnp.concatenate([bias_t, jnp.zeros((LANE - n_sel, tq), F32)], axis=0).astype(BF16)
    bias = jnp.concatenate([bias_t] * gh, axis=1)
    for sp in range(ROW_GROUPS):
        qaug_ref[sp] = jnp.concatenate([qt_ref[sp], bias], axis=0)

    def scores(sp, k0):
        return jnp.dot(kaug_ref[pl.ds(k0, KEY_CHUNK), :], qaug_ref[sp], preferred_element_type=F32)

    def accumulate(sp, p, alpha, c):
        acc_ref[sp] = alpha * acc_ref[sp] + jnp.dot(vst_ref[c], p, preferred_element_type=F32)

    def soften(sp, s_buf):
        m_old = m_ref[sp]
        m_new = jnp.maximum(m_old, jnp.max(s_buf[sp], axis=0, keepdims=True))
        m_ref[sp] = m_new
        return jnp.exp2(m_old - m_new), jnp.exp2(s_buf[sp] - m_new).astype(BF16)

    n_full = qs // KEY_CHUNK

    def step(c, s_in, p_prev, s_out, p_out):
        k_next = pl.multiple_of((c + 1) * KEY_CHUNK, KEY_CHUNK)
        c_prev = jnp.where(c == 0, n_full, c - 1)
        for sp in range(ROW_GROUPS):
            accumulate(sp, p_prev[sp], a_ref[sp], c_prev)
            if s_out is not None:
                s_out[sp] = scores(sp, k_next)
        for sp in range(ROW_GROUPS):
            a_ref[sp], p_out[sp] = soften(sp, s_in)

    causal_bias = jnp.concatenate([dbias_ref[i % (KEY_CHUNK // tq)]] * gh, axis=1)
    for sp in range(ROW_GROUPS):
        m_ref[sp] = jnp.full((1, rg), NEG_INF, F32)
        acc_ref[sp] = jnp.zeros((dh + ONES_ROWS, rg), F32)
        s_b[sp] = scores(sp, pl.multiple_of(n_full * KEY_CHUNK, KEY_CHUNK)) + causal_bias
        a_ref[sp], p_b[sp] = soften(sp, s_b)
        s_a[sp] = scores(sp, 0)

    wc0 = w0 // WIN_CHUNK
    for sp in range(ROW_GROUPS):
        m_w = jnp.max(sw_ref[sp], axis=0, keepdims=True)
        ew_ref[sp] = jnp.exp2(sw_ref[sp] - m_w).astype(BF16)
    for sp in range(ROW_GROUPS):
        pv = None
        for jw in range(n_wc):
            part = jnp.dot(vwt_ref[wc0 + jw], ew_ref[sp, jw * WIN_CHUNK:(jw + 1) * WIN_CHUNK, :],
                           preferred_element_type=F32)
            pv = part if pv is None else pv + part
        ow_ref[sp] = pv[:dh] / pv[dh:dh + 1]

    def pair(k, carry):
        step(2 * k, s_a, p_b, s_b, p_a)
        step(2 * k + 1, s_b, p_a, s_a, p_b)
        return carry

    n_steps = jnp.maximum(n_full - 1, 0)
    lax.fori_loop(0, n_steps // 2, pair, 0)

    @pl.when(n_full == 0)
    def _():
        for sp in range(ROW_GROUPS):
            accumulate(sp, p_b[sp], a_ref[sp], n_full)

    @pl.when((n_full > 0) & (n_steps % 2 == 0))
    def _():
        step(n_steps, s_a, p_b, None, p_a)
        for sp in range(ROW_GROUPS):
            accumulate(sp, p_a[sp], a_ref[sp], n_steps)

    @pl.when((n_full > 0) & (n_steps % 2 == 1))
    def _():
        step(n_steps - 1, s_a, p_b, s_b, p_a)
        step(n_steps, s_b, p_a, None, p_b)
        for sp in range(ROW_GROUPS):
            accumulate(sp, p_b[sp], a_ref[sp], n_steps)

    gate_t = transposed(jax.nn.sigmoid(gate_ref[...]))
    for g in range(GQA):
        sp, gg = divmod(g, gh)
        ln = slice(gg * tq, (gg + 1) * tq)
        o_s = acc_ref[sp, 0:dh, ln] / acc_ref[sp, dh:dh + 1, ln]
        tot = (gate_t[N_BRANCH * g:N_BRANCH * g + 1, :] * oc_ref[sp, :, ln]
               + gate_t[N_BRANCH * g + 1:N_BRANCH * g + 2, :] * o_s
               + gate_t[N_BRANCH * g + 2:N_BRANCH * g + 3, :] * ow_ref[sp, :, ln])
        o_ref[:, g * dh:(g + 1) * dh] = tot.T


def _attention_t(q, kv4, gates, k_cmp, v_cmp, ov_t, bsz, s, cast=()):
    dh = HEAD_DIM
    tq = Q_BLOCK
    nq = s // tq
    rg = (GQA // ROW_GROUPS) * tq
    n_cmp = k_cmp.shape[1]
    wlen = min(WINDOW + tq, s)
    kv = lambda blk: pl.BlockSpec((s, dh), lambda b, h, i: (b, blk + h))
    cmp_spec = pl.BlockSpec((1, n_cmp, dh), lambda b, h, i: (b * N_KV_HEADS + h, 0, 0))
    cast_specs, cast_shapes = _cast_rider_specs(cast, bsz * N_KV_HEADS * nq,
                                                lambda b, h, i: ((b * N_KV_HEADS + h) * nq + i, 0))
    grp = lambda rows, dt: pltpu.VMEM((ROW_GROUPS, rows, rg), dt)
    return pl.pallas_call(
        functools.partial(_attn_t_kernel, n_cast=len(cast)),
        grid=(bsz, N_KV_HEADS, nq),
        in_specs=[pl.BlockSpec((tq, GQA * dh), lambda b, h, i: (b * nq + i, h)),
                  kv(0), kv(N_KV_HEADS), kv(2 * N_KV_HEADS), kv(3 * N_KV_HEADS),
                  pl.BlockSpec((tq, LANE), lambda b, h, i: (b * nq + i, h)),
                  cmp_spec, cmp_spec,
                  pl.BlockSpec(ov_t.shape, lambda b, h, i: (0, 0)), *cast_specs],
        out_specs=[pl.BlockSpec((tq, GQA * dh), lambda b, h, i: (b * nq + i, h)), *cast_specs],
        out_shape=[jax.ShapeDtypeStruct((bsz * s, N_HEADS * dh), F32), *cast_shapes],
        scratch_shapes=[pltpu.VMEM((s, dh + LANE), BF16),
                        pltpu.VMEM((s // KEY_CHUNK, dh + ONES_ROWS, KEY_CHUNK), BF16),
                        pltpu.VMEM((s, dh), BF16),
                        pltpu.VMEM((s // WIN_CHUNK, dh + ONES_ROWS, WIN_CHUNK), BF16),
                        pltpu.VMEM((KEY_CHUNK // tq, KEY_CHUNK, tq), F32),
                        pltpu.VMEM((WINDOW // tq + 1, wlen, tq), F32),
                        grp(dh, BF16), grp(dh + LANE, BF16),
                        grp(KEY_CHUNK, F32), grp(KEY_CHUNK, F32),
                        grp(KEY_CHUNK, BF16), grp(KEY_CHUNK, BF16),
                        grp(dh + ONES_ROWS, F32), grp(1, F32), grp(1, F32),
                        grp(n_cmp, F32), grp(n_cmp, F32), grp(dh, F32),
                        grp(wlen, F32), grp(wlen, BF16), grp(dh, F32)],
        compiler_params=_cparams(("arbitrary", "arbitrary", "arbitrary")),
        name="nsa_attention",
    )(q, kv4, kv4, kv4, kv4, gates, k_cmp, v_cmp, ov_t, *cast)


def _outproj_kernel(lru_ref, attn_ref, ga_ref, x_ref, w_ref, gf_ref, h_ref, xn_ref):
    c = lru_ref.shape[1]
    an = _rms(attn_ref[...], ga_ref[...]).astype(BF16)
    h = (x_ref[...]
         + jnp.dot(lru_ref[...], w_ref[0:c, :], preferred_element_type=F32)
         + jnp.dot(an, w_ref[c:, :], preferred_element_type=F32))
    h_ref[...] = h
    xn_ref[...] = _rms(h, gf_ref[...]).astype(BF16)


def _out_proj(lru_n, attn, g_attn, x2, w_out, g_ffn, tm):
    t, d = x2.shape
    c = lru_n.shape[1]
    return pl.pallas_call(
        _outproj_kernel,
        grid=(t // tm,),
        in_specs=[pl.BlockSpec((tm, c), lambda i: (i, 0)),
                  pl.BlockSpec((tm, attn.shape[1]), lambda i: (i, 0)),
                  pl.BlockSpec((1, attn.shape[1]), lambda i: (0, 0)),
                  pl.BlockSpec((tm, d), lambda i: (i, 0)),
                  pl.BlockSpec(w_out.shape, lambda i: (0, 0), pipeline_mode=pl.Buffered(1)),
                  pl.BlockSpec((1, d), lambda i: (0, 0))],
        out_specs=[pl.BlockSpec((tm, d), lambda i: (i, 0)),
                   pl.BlockSpec((tm, d), lambda i: (i, 0))],
        out_shape=[jax.ShapeDtypeStruct((t, d), F32), jax.ShapeDtypeStruct((t, d), BF16)],
        compiler_params=_cparams(("arbitrary",)),
        name="out_proj",
    )(lru_n, attn, g_attn, x2, w_out, g_ffn)


def _ffn_kernel(xn_ref, wu_ref, wv_ref, cw_ref, cb_ref, wd_ref, h_ref, g_ref, o_ref,
                halo_ref, upad_ref, *, tiles_per_seq, final_norm):
    i = pl.program_id(0)
    j = pl.program_id(1)
    tm = xn_ref.shape[0]

    @pl.when(j == 0)
    def _():
        o_ref[...] = h_ref[...]

    xn = xn_ref[...]
    u = jnp.dot(xn, wu_ref[...], preferred_element_type=F32)
    v = jnp.dot(xn, wv_ref[...], preferred_element_type=F32)
    first = (i % tiles_per_seq) == 0
    upad_ref[0:SUBLANE, :] = jnp.where(first, 0.0, halo_ref[j])
    upad_ref[SUBLANE:, :] = u
    halo_ref[j] = u[tm - SUBLANE:, :]
    uc = cb_ref[...] + cw_ref[FFN_CONV - 1:FFN_CONV, :] * u
    for k in range(FFN_CONV - 1):
        off = SUBLANE - (FFN_CONV - 1) + k
        uc = uc + cw_ref[k:k + 1, :] * upad_ref[off:off + tm, :]
    act = (jax.nn.gelu(uc) * v).astype(BF16)
    o_ref[...] += jnp.dot(act, wd_ref[...], preferred_element_type=F32)

    if final_norm:
        @pl.when(j == pl.num_programs(1) - 1)
        def _():
            o_ref[...] = _rms(o_ref[...], g_ref[...])


def _ffn(xn, h1, w_up, cw, cb, w_down, g_final, final_norm, s, tm, tf):
    t, d = xn.shape
    dff = w_down.shape[0]
    nj = dff // tf
    return pl.pallas_call(
        functools.partial(_ffn_kernel, tiles_per_seq=s // tm, final_norm=final_norm),
        grid=(t // tm, nj),
        in_specs=[pl.BlockSpec((tm, d), lambda i, j: (i, 0)),
                  pl.BlockSpec((d, tf), lambda i, j: (0, j)),
                  pl.BlockSpec((d, tf), lambda i, j: (0, nj + j)),
                  pl.BlockSpec((FFN_CONV, tf), lambda i, j: (0, j)),
                  pl.BlockSpec((1, tf), lambda i, j: (0, j)),
                  pl.BlockSpec((tf, d), lambda i, j: (j, 0)),
                  pl.BlockSpec((tm, d), lambda i, j: (i, 0)),
                  pl.BlockSpec((1, d), lambda i, j: (0, 0))],
        out_specs=pl.BlockSpec((tm, d), lambda i, j: (i, 0)),
        out_shape=jax.ShapeDtypeStruct((t, d), F32),
        scratch_shapes=[pltpu.VMEM((nj, SUBLANE, tf), F32),
                        pltpu.VMEM((tm + SUBLANE, tf), F32)],
        compiler_params=_cparams(("arbitrary", "arbitrary")),
        name="conv_ffn",
    )(xn, w_up, w_up, cw, cb, w_down, h1, g_final)


def _overlap_t(n_cmp, n_sel):
    c0 = jnp.arange(n_cmp)[None, :] * CMP_STRIDE
    s0 = jnp.arange(n_sel)[:, None] * SEL_BLOCK
    ov = jnp.clip(jnp.minimum(c0 + CMP_BLOCK, s0 + SEL_BLOCK) - jnp.maximum(c0, s0), 0)
    return ov.astype(F32) / CMP_BLOCK


def kernel(x, g_mix, w_in, lru_conv_w, lru_conv_b, lru_wa, lru_ba, lru_wx, lru_bx, lru_lambda, cmp_pe_k, cmp_w1_k, cmp_b1_k, cmp_w2_k, cmp_pe_v, cmp_w1_v, cmp_b1_v, cmp_w2_v, g_lru_out, g_attn_out, w_out, g_ffn, w_up, ffn_conv_w, ffn_conv_b, w_down, g_final):
    bsz, s, d = x.shape
    depth = w_in.shape[0]
    c_lru = lru_conv_w.shape[2]
    attn_w = N_HEADS * HEAD_DIM
    kv_w = N_KV_HEADS * HEAD_DIM
    n_gate = N_BRANCH * GQA
    main_w = 2 * c_lru + attn_w + 6 * kv_w
    segments = ((0, 2 * c_lru, F32, None),
                (2 * c_lru, attn_w, BF16, HEAD_DIM ** -0.5 * LOG2_E),
                (2 * c_lru + attn_w, 2 * kv_w, F32, None),
                (2 * c_lru + attn_w + 2 * kv_w, 4 * kv_w, BF16, None))
    ov_t = _overlap_t(s // CMP_STRIDE, s // SEL_BLOCK)
    row = lambda a: a.reshape(1, -1)

    h = x.reshape(bsz * s, d)
    for l in range(depth):
        w_main = w_in[l].astype(BF16)
        gpad = jnp.zeros((d, LANE - n_gate), F32)
        w_gate = jnp.concatenate([p for hh in range(N_KV_HEADS)
                                  for p in (w_in[l][:, main_w + hh * n_gate:main_w + (hh + 1) * n_gate], gpad)],
                                 axis=1).astype(BF16)
        lru_in, q, kvc, kv4, gates = _in_proj(h, row(g_mix[l]), w_main, segments, w_gate, tm=ROW_TILE)
        lru_n = _lru(lru_in, bsz, s, lru_conv_w[l], row(lru_conv_b[l]), lru_wa[l].astype(BF16),
                        row(lru_ba[l]), lru_wx[l].astype(BF16), row(lru_bx[l]), row(lru_lambda[l]),
                        row(g_lru_out[l]), ts=LRU_TILE)
        pk = (cmp_pe_k[l].reshape(1, -1), cmp_w1_k[l].astype(BF16), row(cmp_b1_k[l]), cmp_w2_k[l].astype(BF16))
        pv = (cmp_pe_v[l].reshape(1, -1), cmp_w1_v[l].astype(BF16), row(cmp_b1_v[l]), cmp_w2_v[l].astype(BF16))
        k_cmp, v_cmp = _compress(kvc, bsz, s, pk, pv)
        attn, w_out_b, w_down_b, w_up_b = _attention_t(q, kv4, gates, k_cmp, v_cmp, ov_t, bsz, s,
                                                     cast=(w_out[l], w_down[l], w_up[l]))
        h1, xn = _out_proj(lru_n, attn, row(g_attn_out[l]), h, w_out_b, row(g_ffn[l]), tm=ROW_TILE)
        h = _ffn(xn, h1, w_up_b, ffn_conv_w[l], row(ffn_conv_b[l]), w_down_b,
                 row(g_final), l == depth - 1, s, tm=ROW_TILE, tf=FF_TILE)
    return h.reshape(bsz, s, d)
```

```python
import functools

import jax
import jax.numpy as jnp
from jax import lax
from jax.experimental import pallas as pl
from jax.experimental.pallas import tpu as pltpu

F32 = jnp.float32
BF16 = jnp.bfloat16

LANE = 128
SUBLANE = 8
VMEM_LIMIT = 56 * 1024 * 1024

LRU_HEADS = 8
LRU_CONV = 4
LRU_C = 8.0
N_HEADS = 8
N_KV_HEADS = 2
GQA = N_HEADS // N_KV_HEADS
HEAD_DIM = 128
N_BRANCH = 3
CMP_BLOCK = 32
CMP_STRIDE = 16
SEL_BLOCK = 64
N_SEL = 16
N_LOCAL_SEL = 2
WINDOW = 512
Q_BLOCK = 256
FFN_CONV = 3
EPS = 1e-6
NEG_INF = -1e30
FORCE_SCORE = 1e4
LOG2_E = 1.4426950408889634

ROW_TILE = 512
FF_TILE = 1024
LRU_TILE = 256


def _rms(x, g):
    return x * lax.rsqrt(jnp.mean(x * x, axis=-1, keepdims=True) + EPS) * g


def _cparams(sem):
    return pltpu.CompilerParams(dimension_semantics=sem, vmem_limit_bytes=VMEM_LIMIT)


def _inproj_kernel(x_ref, g_ref, w_ref, wg_ref, *o_refs, segments):
    xn = _rms(x_ref[...], g_ref[...]).astype(BF16)
    for (c0, width, scale), o_ref in zip(segments, o_refs):
        y = jnp.dot(xn, w_ref[:, c0:c0 + width], preferred_element_type=F32)
        o_ref[...] = (y if scale is None else y * scale).astype(o_ref.dtype)
    o_refs[-1][...] = jnp.dot(xn, wg_ref[...], preferred_element_type=F32)


def _in_proj(x2, g, w_all, segments, w_gate, tm):
    t, d = x2.shape
    ng = w_gate.shape[1]
    n = max(c0 + width for c0, width, _, _ in segments)
    resident = lambda shape: pl.BlockSpec(shape, lambda i: (0, 0), pipeline_mode=pl.Buffered(1))
    return pl.pallas_call(
        functools.partial(_inproj_kernel, segments=tuple((c0, width, sc) for c0, width, _, sc in segments)),
        grid=(t // tm,),
        in_specs=[pl.BlockSpec((tm, d), lambda i: (i, 0)),
                  pl.BlockSpec((1, d), lambda i: (0, 0)),
                  resident((d, n)), resident(w_gate.shape)],
        out_specs=[*[pl.BlockSpec((tm, width), lambda i: (i, 0)) for _, width, _, _ in segments],
                   pl.BlockSpec((tm, ng), lambda i: (i, 0))],
        out_shape=[*[jax.ShapeDtypeStruct((t, width), dt) for _, width, dt, _ in segments],
                   jax.ShapeDtypeStruct((t, ng), F32)],
        compiler_params=_cparams(("arbitrary",)),
        name="in_proj",
    )(x2, g, w_all, w_gate)


def _cast_rider_specs(arrays, n_steps, index):
    specs, shapes = [], []
    for a in arrays:
        rows = a.shape[0] // n_steps
        assert rows * n_steps == a.shape[0] and rows % (2 * SUBLANE) == 0
        specs.append(pl.BlockSpec((rows, a.shape[1]), index))
        shapes.append(jax.ShapeDtypeStruct(a.shape, BF16))
    return specs, shapes


def _lru_kernel(x_ref, gate_ref, cw_ref, cb_ref, wa_ref, ba_ref, wx_ref, bx_ref, lam_ref, g_ref,
                o_ref, xpad_ref, carry_ref, a_ref, b_ref):
    ts, c = x_ref.shape
    hb = c // LRU_HEADS

    @pl.when(pl.program_id(1) == 0)
    def _():
        xpad_ref[...] = jnp.zeros_like(xpad_ref)
        carry_ref[...] = jnp.zeros_like(carry_ref)

    xpad_ref[0:SUBLANE, :] = xpad_ref[ts:ts + SUBLANE, :]
    xpad_ref[SUBLANE:, :] = x_ref[...]
    u = cb_ref[...] + cw_ref[LRU_CONV - 1:LRU_CONV, :] * x_ref[...]
    for k in range(LRU_CONV - 1):
        off = SUBLANE - (LRU_CONV - 1) + k
        u = u + cw_ref[k:k + 1, :] * xpad_ref[off:off + ts, :]

    ub = u.astype(BF16)
    r_parts, i_parts = [], []
    for h in range(LRU_HEADS):
        uh = ub[:, h * hb:(h + 1) * hb]
        r_parts.append(jnp.dot(uh, wa_ref[h], preferred_element_type=F32))
        i_parts.append(jnp.dot(uh, wx_ref[h], preferred_element_type=F32))
    r = jax.nn.sigmoid(jnp.concatenate(r_parts, axis=1) + ba_ref[...])
    ig = jax.nn.sigmoid(jnp.concatenate(i_parts, axis=1) + bx_ref[...])
    nl = -lam_ref[...]
    softplus = jnp.maximum(nl, 0.0) + jnp.log1p(jnp.exp(-jnp.abs(nl)))
    log_a = (-LRU_C) * r * softplus
    a = jnp.exp(log_a)
    a_ref[...] = a
    z = -jnp.tanh(log_a) * (a * a + 1.0)
    root = jnp.where(z > 0.0, z * lax.rsqrt(z), 0.0)
    b_ref[...] = root * (ig * u)

    row = lax.broadcasted_iota(jnp.int32, (SUBLANE, c), 0)

    def group(gi, carry):
        r0 = pl.multiple_of(gi * SUBLANE, SUBLANE)
        a8 = a_ref[pl.ds(r0, SUBLANE), :]
        b8 = b_ref[pl.ds(r0, SUBLANE), :]
        for d in (1, 2, 4):
            a_sh = pltpu.roll(a8, d, 0)
            b_sh = pltpu.roll(b8, d, 0)
            m = row >= d
            b8 = jnp.where(m, a8 * b_sh + b8, b8)
            a8 = jnp.where(m, a8 * a_sh, a8)
        h8 = a8 * carry + b8
        b_ref[pl.ds(r0, SUBLANE), :] = h8
        return jnp.broadcast_to(h8[SUBLANE - 1:SUBLANE, :], (SUBLANE, c))

    carry_ref[...] = lax.fori_loop(0, ts // SUBLANE, group, carry_ref[...])

    y = b_ref[...] * jax.nn.gelu(gate_ref[...])
    o_ref[...] = _rms(y, g_ref[...]).astype(o_ref.dtype)


def _lru(proj, bsz, s, cw, cb, wa, ba, wx, bx, lam, g, ts):
    c = cw.shape[1]
    ns = s // ts
    row = lambda b, i: b * ns + i
    vec = pl.BlockSpec((1, c), lambda b, i: (0, 0))
    mat = pl.BlockSpec(wa.shape, lambda b, i: (0, 0, 0))
    return pl.pallas_call(
        _lru_kernel,
        grid=(bsz, ns),
        in_specs=[pl.BlockSpec((ts, c), lambda b, i: (row(b, i), 0)),
                  pl.BlockSpec((ts, c), lambda b, i: (row(b, i), 1)),
                  pl.BlockSpec((LRU_CONV, c), lambda b, i: (0, 0)), vec,
                  mat, vec, mat, vec, vec, vec],
        out_specs=pl.BlockSpec((ts, c), lambda b, i: (row(b, i), 0)),
        out_shape=jax.ShapeDtypeStruct((bsz * s, c), BF16),
        scratch_shapes=[pltpu.VMEM((ts + SUBLANE, c), F32),
                        pltpu.VMEM((SUBLANE, c), F32),
                        pltpu.VMEM((ts, c), F32),
                        pltpu.VMEM((ts, c), F32)],
        compiler_params=_cparams(("arbitrary", "arbitrary")),
        name="rg_lru",
    )(proj, proj, cw, cb, wa, ba, wx, bx, lam, g)


def _compress_one(kv_ref, pe_ref, w1_ref, b1_ref, w2_ref):
    s, dh = kv_ref.shape
    n = s // CMP_STRIDE
    half = CMP_STRIDE * dh
    xa = jnp.concatenate([kv_ref[pl.ds(l, n, stride=CMP_STRIDE), :] for l in range(CMP_STRIDE)],
                         axis=1).astype(BF16)
    lo = jnp.dot(xa, w1_ref[0:half, :], preferred_element_type=F32)
    hi = jnp.dot(xa, w1_ref[half:2 * half, :], preferred_element_type=F32)
    pe = jnp.broadcast_to(pe_ref[...], (SUBLANE, 2 * half)).astype(BF16)
    pe_term = jnp.dot(pe, w1_ref[...], preferred_element_type=F32)[0:1, :]
    pre = lo + pltpu.roll(hi, n - 1, 0) + pe_term + b1_ref[...]
    return jnp.dot(jax.nn.gelu(pre).astype(BF16), w2_ref[...], preferred_element_type=F32)


def _compress_kernel(kc_ref, vc_ref, pek_ref, w1k_ref, b1k_ref, w2k_ref,
                     pev_ref, w1v_ref, b1v_ref, w2v_ref, ko_ref, vo_ref):
    ko_ref[0] = _compress_one(kc_ref, pek_ref, w1k_ref, b1k_ref, w2k_ref)
    vo_ref[0] = _compress_one(vc_ref, pev_ref, w1v_ref, b1v_ref, w2v_ref)


def _compress(kvc, bsz, s, pk, pv):
    dh = HEAD_DIM
    kc_blk, vc_blk = 0, N_KV_HEADS
    n = s // CMP_STRIDE
    full = lambda a: pl.BlockSpec(a.shape, lambda b, h: (0,) * a.ndim)
    out = pl.BlockSpec((1, n, dh), lambda b, h: (b * N_KV_HEADS + h, 0, 0))
    shp = jax.ShapeDtypeStruct((bsz * N_KV_HEADS, n, dh), F32)
    return pl.pallas_call(
        _compress_kernel,
        grid=(bsz, N_KV_HEADS),
        in_specs=[pl.BlockSpec((s, dh), lambda b, h: (b, kc_blk + h)),
                  pl.BlockSpec((s, dh), lambda b, h: (b, vc_blk + h)),
                  *[full(a) for a in pk], *[full(a) for a in pv]],
        out_specs=[out, out],
        out_shape=[shp, shp],
        compiler_params=_cparams(("arbitrary", "arbitrary")),
        name="compress_kv",
    )(kvc, kvc, *pk, *pv)


KEY_CHUNK = 512
ROW_GROUPS = 4
ONES_ROWS = 16
WIN_CHUNK = 256


def _attn_t_kernel(q_ref, ksel_ref, vsel_ref, kw_ref, vw_ref, gate_ref, kcmp_ref, vcmp_ref, ovt_ref, *rest, n_cast):
    cast_in, o_ref, cast_out = rest[:n_cast], rest[n_cast], rest[n_cast + 1:2 * n_cast + 1]
    (kaug_ref, vst_ref, kwb_ref, vwt_ref, dbias_ref, wbias_ref,
     qt_ref, qaug_ref, s_a, s_b, p_a, p_b, acc_ref, m_ref, a_ref,
     sc_ref, pc_ref, oc_ref, sw_ref, ew_ref, ow_ref) = rest[2 * n_cast + 1:]
    for src, dst in zip(cast_in, cast_out):
        dst[...] = src[...].astype(dst.dtype)
    i = pl.program_id(2)
    s_len, dh = ksel_ref.shape
    tq = q_ref.shape[0]
    n_cmp = kcmp_ref.shape[1]
    n_sel = s_len // SEL_BLOCK
    gh = GQA // ROW_GROUPS
    rg = gh * tq
    wlen = wbias_ref.shape[1]
    n_wc = wlen // WIN_CHUNK

    def transposed(x):
        return x.astype(F32).T

    @pl.when(i == 0)
    def _():
        kaug_ref[:, 0:dh] = ksel_ref[...].astype(BF16)
        kblk = lax.broadcasted_iota(jnp.int32, (s_len, LANE), 0) // SEL_BLOCK
        lane = lax.broadcasted_iota(jnp.int32, (s_len, LANE), 1)
        kaug_ref[:, dh:dh + LANE] = jnp.where(kblk == lane, 1.0, 0.0).astype(BF16)
        kwb_ref[...] = kw_ref[...].astype(BF16)
        for c in range(s_len // KEY_CHUNK):
            vst_ref[c, 0:dh, :] = transposed(vsel_ref[c * KEY_CHUNK:(c + 1) * KEY_CHUNK, :]).astype(BF16)
            vst_ref[c, dh:dh + ONES_ROWS, :] = jnp.ones((ONES_ROWS, KEY_CHUNK), BF16)
        for c in range(s_len // WIN_CHUNK):
            vwt_ref[c, 0:dh, :] = transposed(vw_ref[c * WIN_CHUNK:(c + 1) * WIN_CHUNK, :]).astype(BF16)
            vwt_ref[c, dh:dh + ONES_ROWS, :] = jnp.ones((ONES_ROWS, WIN_CHUNK), BF16)
        tok_d = lax.broadcasted_iota(jnp.int32, (KEY_CHUNK, tq), 1)
        key_d = lax.broadcasted_iota(jnp.int32, (KEY_CHUNK, tq), 0)
        for v in range(dbias_ref.shape[0]):
            dbias_ref[v] = jnp.where(key_d <= v * tq + tok_d, 0.0, NEG_INF)
        tok_w = lax.broadcasted_iota(jnp.int32, (wlen, tq), 1)
        key_w = lax.broadcasted_iota(jnp.int32, (wlen, tq), 0)
        for v in range(wbias_ref.shape[0]):
            d = v * tq + tok_w - key_w
            wbias_ref[v] = jnp.where(d >= 0, jnp.where(d < WINDOW, 0.0, NEG_INF), NEG_INF)

    qs = i * tq
    t_lane = qs + (lax.broadcasted_iota(jnp.int32, (1, rg), 1) & (tq - 1))
    for sp in range(ROW_GROUPS):
        qt_ref[sp] = jnp.concatenate([transposed(q_ref[:, g * dh:(g + 1) * dh])
                                      for g in range(sp * gh, (sp + 1) * gh)], axis=1).astype(BF16)

    w0 = pl.multiple_of(jnp.maximum(qs + tq - wlen, 0), tq)
    band_bias = jnp.concatenate([wbias_ref[jnp.minimum(i, wbias_ref.shape[0] - 1)]] * gh, axis=1)
    kcb = kcmp_ref[0].astype(BF16)
    vct = transposed(vcmp_ref[0]).astype(BF16)
    vis_c = (lax.broadcasted_iota(jnp.int32, (n_cmp, rg), 0) * CMP_STRIDE + (CMP_BLOCK - 1)) <= t_lane
    for sp in range(ROW_GROUPS):
        sc_ref[sp] = jnp.where(vis_c, jnp.dot(kcb, qt_ref[sp], preferred_element_type=F32), NEG_INF)
        sw_ref[sp] = jnp.dot(kwb_ref[pl.ds(w0, wlen), :], qt_ref[sp], preferred_element_type=F32) + band_bias
    for sp in range(ROW_GROUPS):
        m_c = jnp.max(sc_ref[sp], axis=0, keepdims=True)
        e_c = jnp.where(vis_c, jnp.exp2(sc_ref[sp] - m_c), 0.0)
        pc_ref[sp] = e_c
        den_c = jnp.sum(e_c, axis=0, keepdims=True)
        pc_ref[sp] = pc_ref[sp] / jnp.where(den_c > 0.0, den_c, 1.0)
        oc_ref[sp] = jnp.dot(vct, pc_ref[sp].astype(BF16), preferred_element_type=F32)
    p_sum = None
    for sp in range(ROW_GROUPS):
        for g in range(gh):
            pg = pc_ref[sp, :, g * tq:(g + 1) * tq]
            p_sum = pg if p_sum is None else p_sum + pg

    ov_b = ovt_ref[...].astype(BF16)
    imp_t, rest_p = None, p_sum
    for _ in range(3):
        piece = rest_p.astype(BF16)
        part = jnp.dot(ov_b, piece, preferred_element_type=F32)
        imp_t = part if imp_t is None else imp_t + part
        rest_p = rest_p - piece.astype(F32)
    blk = lax.broadcasted_iota(jnp.int32, (n_sel, tq), 0)
    cur = (qs + lax.broadcasted_iota(jnp.int32, (n_sel, tq), 1)) // SEL_BLOCK
    age = cur - blk
    score = jnp.where(blk == 0, FORCE_SCORE,
                      jnp.where(age < 0, -1.0, jnp.where(age < N_LOCAL_SEL, FORCE_SCORE, imp_t)))
    parts = [jnp.zeros((n_sel, tq), jnp.int32) for _ in range(4)]
    for j in range(n_sel):
        rj = score[j:j + 1, :]
        parts[j % 4] = parts[j % 4] + jnp.where(blk > j, jnp.where(rj >= score, 1, 0),
                                                jnp.where(rj > score, 1, 0))
    rank = (parts[0] + parts[1]) + (parts[2] + parts[3])
    bias_t = jnp.where(rank < min(N_SEL, n_sel), 0.0, NEG_INF)
    bias_t = jnp.concatenate([bias_t, jnp.zeros((LANE - n_sel, tq), F32)], axis=0).astype(BF16)
    bias = jnp.concatenate([bias_t] * gh, axis=1)
    for sp in range(ROW_GROUPS):
        qaug_ref[sp] = jnp.concatenate([qt_ref[sp], bias], axis=0)

    def scores(sp, k0):
        return jnp.dot(kaug_ref[pl.ds(k0, KEY_CHUNK), :], qaug_ref[sp], preferred_element_type=F32)

    def accumulate(sp, p, alpha, c):
        acc_ref[sp] = alpha * acc_ref[sp] + jnp.dot(vst_ref[c], p, preferred_element_type=F32)

    def soften(sp, s_buf):
        m_old = m_ref[sp]
        m_new = jnp.maximum(m_old, jnp.max(s_buf[sp], axis=0, keepdims=True))
        m_ref[sp] = m_new
        return jnp.exp2(m_old - m_new), jnp.exp2(s_buf[sp] - m_new).astype(BF16)

    n_full = qs // KEY_CHUNK

    def step(c, s_in, p_prev, s_out, p_out):
        k_next = pl.multiple_of((c + 1) * KEY_CHUNK, KEY_CHUNK)
        c_prev = jnp.where(c == 0, n_full, c - 1)
        for sp in range(ROW_GROUPS):
            accumulate(sp, p_prev[sp], a_ref[sp], c_prev)
            if s_out is not None:
                s_out[sp] = scores(sp, k_next)
        for sp in range(ROW_GROUPS):
            a_ref[sp], p_out[sp] = soften(sp, s_in)

    causal_bias = jnp.concatenate([dbias_ref[i % (KEY_CHUNK // tq)]] * gh, axis=1)
    for sp in range(ROW_GROUPS):
        m_ref[sp] = jnp.full((1, rg), NEG_INF, F32)
        acc_ref[sp] = jnp.zeros((dh + ONES_ROWS, rg), F32)
        s_b[sp] = scores(sp, pl.multiple_of(n_full * KEY_CHUNK, KEY_CHUNK)) + causal_bias
        a_ref[sp], p_b[sp] = soften(sp, s_b)
        s_a[sp] = scores(sp, 0)

    wc0 = w0 // WIN_CHUNK
    for sp in range(ROW_GROUPS):
        m_w = jnp.max(sw_ref[sp], axis=0, keepdims=True)
        ew_ref[sp] = jnp.exp2(sw_ref[sp] - m_w).astype(BF16)
    for sp in range(ROW_GROUPS):
        pv = None
        for jw in range(n_wc):
            part = jnp.dot(vwt_ref[wc0 + jw], ew_ref[sp, jw * WIN_CHUNK:(jw + 1) * WIN_CHUNK, :],
                           preferred_element_type=F32)
            pv = part if pv is None else pv + part
        ow_ref[sp] = pv[:dh] / pv[dh:dh + 1]

    def pair(k, carry):
        step(2 * k, s_a, p_b, s_b, p_a)
        step(2 * k + 1, s_b, p_a, s_a, p_b)
        return carry

    n_steps = jnp.maximum(n_full - 1, 0)
    lax.fori_loop(0, n_steps // 2, pair, 0)

    @pl.when(n_full == 0)
    def _():
        for sp in range(ROW_GROUPS):
            accumulate(sp, p_b[sp], a_ref[sp], n_full)

    @pl.when((n_full > 0) & (n_steps % 2 == 0))
    def _():
        step(n_steps, s_a, p_b, None, p_a)
        for sp in range(ROW_GROUPS):
            accumulate(sp, p_a[sp], a_ref[sp], n_steps)

    @pl.when((n_full > 0) & (n_steps % 2 == 1))
    def _():
        step(n_steps - 1, s_a, p_b, s_b, p_a)
        step(n_steps, s_b, p_a, None, p_b)
        for sp in range(ROW_GROUPS):
            accumulate(sp, p_b[sp], a_ref[sp], n_steps)

    gate_t = transposed(jax.nn.sigmoid(gate_ref[...]))
    for g in range(GQA):
        sp, gg = divmod(g, gh)
        ln = slice(gg * tq, (gg + 1) * tq)
        o_s = acc_ref[sp, 0:dh, ln] / acc_ref[sp, dh:dh + 1, ln]
        tot = (gate_t[N_BRANCH * g:N_BRANCH * g + 1, :] * oc_ref[sp, :, ln]
               + gate_t[N_BRANCH * g + 1:N_BRANCH * g + 2, :] * o_s
               + gate_t[N_BRANCH * g + 2:N_BRANCH * g + 3, :] * ow_ref[sp, :, ln])
        o_ref[:, g * dh:(g + 1) * dh] = tot.T


def _attention_t(q, kv4, gates, k_cmp, v_cmp, ov_t, bsz, s, cast=()):
    dh = HEAD_DIM
    tq = Q_BLOCK
    nq = s // tq
    rg = (GQA // ROW_GROUPS) * tq
    n_cmp = k_cmp.shape[1]
    wlen = min(WINDOW + tq, s)
    kv = lambda blk: pl.BlockSpec((s, dh), lambda b, h, i: (b, blk + h))
    cmp_spec = pl.BlockSpec((1, n_cmp, dh), lambda b, h, i: (b * N_KV_HEADS + h, 0, 0))
    cast_specs, cast_shapes = _cast_rider_specs(cast, bsz * N_KV_HEADS * nq,
                                                lambda b, h, i: ((b * N_KV_HEADS + h) * nq + i, 0))
    grp = lambda rows, dt: pltpu.VMEM((ROW_GROUPS, rows, rg), dt)
    return pl.pallas_call(
        functools.partial(_attn_t_kernel, n_cast=len(cast)),
        grid=(bsz, N_KV_HEADS, nq),
        in_specs=[pl.BlockSpec((tq, GQA * dh), lambda b, h, i: (b * nq + i, h)),
                  kv(0), kv(N_KV_HEADS), kv(2 * N_KV_HEADS), kv(3 * N_KV_HEADS),
                  pl.BlockSpec((tq, LANE), lambda b, h, i: (b * nq + i, h)),
                  cmp_spec, cmp_spec,
                  pl.BlockSpec(ov_t.shape, lambda b, h, i: (0, 0)), *cast_specs],
        out_specs=[pl.BlockSpec((tq, GQA * dh), lambda b, h, i: (b * nq + i, h)), *cast_specs],
        out_shape=[jax.ShapeDtypeStruct((bsz * s, N_HEADS * dh), F32), *cast_shapes],
        scratch_shapes=[pltpu.VMEM((s, dh + LANE), BF16),
                        pltpu.VMEM((s // KEY_CHUNK, dh + ONES_ROWS, KEY_CHUNK), BF16),
                        pltpu.VMEM((s, dh), BF16),
                        pltpu.VMEM((s // WIN_CHUNK, dh + ONES_ROWS, WIN_CHUNK), BF16),
                        pltpu.VMEM((KEY_CHUNK // tq, KEY_CHUNK, tq), F32),
                        pltpu.VMEM((WINDOW // tq + 1, wlen, tq), F32),
                        grp(dh, BF16), grp(dh + LANE, BF16),
                        grp(KEY_CHUNK, F32), grp(KEY_CHUNK, F32),
                        grp(KEY_CHUNK, BF16), grp(KEY_CHUNK, BF16),
                        grp(dh + ONES_ROWS, F32), grp(1, F32), grp(1, F32),
                        grp(n_cmp, F32), grp(n_cmp, F32), grp(dh, F32),
                        grp(wlen, F32), grp(wlen, BF16), grp(dh, F32)],
        compiler_params=_cparams(("arbitrary", "arbitrary", "arbitrary")),
        name="nsa_attention",
    )(q, kv4, kv4, kv4, kv4, gates, k_cmp, v_cmp, ov_t, *cast)


def _outproj_kernel(lru_ref, attn_ref, ga_ref, x_ref, w_ref, gf_ref, h_ref, xn_ref):
    c = lru_ref.shape[1]
    an = _rms(attn_ref[...], ga_ref[...]).astype(BF16)
    h = (x_ref[...]
         + jnp.dot(lru_ref[...], w_ref[0:c, :], preferred_element_type=F32)
         + jnp.dot(an, w_ref[c:, :], preferred_element_type=F32))
    h_ref[...] = h
    xn_ref[...] = _rms(h, gf_ref[...]).astype(BF16)


def _out_proj(lru_n, attn, g_attn, x2, w_out, g_ffn, tm):
    t, d = x2.shape
    c = lru_n.shape[1]
    return pl.pallas_call(
        _outproj_kernel,
        grid=(t // tm,),
        in_specs=[pl.BlockSpec((tm, c), lambda i: (i, 0)),
                  pl.BlockSpec((tm, attn.shape[1]), lambda i: (i, 0)),
                  pl.BlockSpec((1, attn.shape[1]), lambda i: (0, 0)),
                  pl.BlockSpec((tm, d), lambda i: (i, 0)),
                  pl.BlockSpec(w_out.shape, lambda i: (0, 0), pipeline_mode=pl.Buffered(1)),
                  pl.BlockSpec((1, d), lambda i: (0, 0))],
        out_specs=[pl.BlockSpec((tm, d), lambda i: (i, 0)),
                   pl.BlockSpec((tm, d), lambda i: (i, 0))],
        out_shape=[jax.ShapeDtypeStruct((t, d), F32), jax.ShapeDtypeStruct((t, d), BF16)],
        compiler_params=_cparams(("arbitrary",)),
        name="out_proj",
    )(lru_n, attn, g_attn, x2, w_out, g_ffn)


def _ffn_kernel(xn_ref, wu_ref, wv_ref, cw_ref, cb_ref, wd_ref, h_ref, g_ref, o_ref,
                halo_ref, upad_ref, *, tiles_per_seq, final_norm):
    i = pl.program_id(0)
    j = pl.program_id(1)
    tm = xn_ref.shape[0]

    @pl.when(j == 0)
    def _():
        o_ref[...] = h_ref[...]

    xn = xn_ref[...]
    u = jnp.dot(xn, wu_ref[...], preferred_element_type=F32)
    v = jnp.dot(xn, wv_ref[...], preferred_element_type=F32)
    first = (i % tiles_per_seq) == 0
    upad_ref[0:SUBLANE, :] = jnp.where(first, 0.0, halo_ref[j])
    upad_ref[SUBLANE:, :] = u
    halo_ref[j] = u[tm - SUBLANE:, :]
    uc = cb_ref[...] + cw_ref[FFN_CONV - 1:FFN_CONV, :] * u
    for k in range(FFN_CONV - 1):
        off = SUBLANE - (FFN_CONV - 1) + k
        uc = uc + cw_ref[k:k + 1, :] * upad_ref[off:off + tm, :]
    act = (jax.nn.gelu(uc) * v).astype(BF16)
    o_ref[...] += jnp.dot(act, wd_ref[...], preferred_element_type=F32)

    if final_norm:
        @pl.when(j == pl.num_programs(1) - 1)
        def _():
            o_ref[...] = _rms(o_ref[...], g_ref[...])


def _ffn(xn, h1, w_up, cw, cb, w_down, g_final, final_norm, s, tm, tf):
    t, d = xn.shape
    dff = w_down.shape[0]
    nj = dff // tf
    return pl.pallas_call(
        functools.partial(_ffn_kernel, tiles_per_seq=s // tm, final_norm=final_norm),
        grid=(t // tm, nj),
        in_specs=[pl.BlockSpec((tm, d), lambda i, j: (i, 0)),
                  pl.BlockSpec((d, tf), lambda i, j: (0, j)),
                  pl.BlockSpec((d, tf), lambda i, j: (0, nj + j)),
                  pl.BlockSpec((FFN_CONV, tf), lambda i, j: (0, j)),
                  pl.BlockSpec((1, tf), lambda i, j: (0, j)),
                  pl.BlockSpec((tf, d), lambda i, j: (j, 0)),
                  pl.BlockSpec((tm, d), lambda i, j: (i, 0)),
                  pl.BlockSpec((1, d), lambda i, j: (0, 0))],
        out_specs=pl.BlockSpec((tm, d), lambda i, j: (i, 0)),
        out_shape=jax.ShapeDtypeStruct((t, d), F32),
        scratch_shapes=[pltpu.VMEM((nj, SUBLANE, tf), F32),
                        pltpu.VMEM((tm + SUBLANE, tf), F32)],
        compiler_params=_cparams(("arbitrary", "arbitrary")),
        name="conv_ffn",
    )(xn, w_up, w_up, cw, cb, w_down, h1, g_final)


def _overlap_t(n_cmp, n_sel):
    c0 = jnp.arange(n_cmp)[None, :] * CMP_STRIDE
    s0 = jnp.arange(n_sel)[:, None] * SEL_BLOCK
    ov = jnp.clip(jnp.minimum(c0 + CMP_BLOCK, s0 + SEL_BLOCK) - jnp.maximum(c0, s0), 0)
    return ov.astype(F32) / CMP_BLOCK


def kernel(x, g_mix, w_in, lru_conv_w, lru_conv_b, lru_wa, lru_ba, lru_wx, lru_bx, lru_lambda, cmp_pe_k, cmp_w1_k, cmp_b1_k, cmp_w2_k, cmp_pe_v, cmp_w1_v, cmp_b1_v, cmp_w2_v, g_lru_out, g_attn_out, w_out, g_ffn, w_up, ffn_conv_w, ffn_conv_b, w_down, g_final):
    bsz, s, d = x.shape
    depth = w_in.shape[0]
    c_lru = lru_conv_w.shape[2]
    attn_w = N_HEADS * HEAD_DIM
    kv_w = N_KV_HEADS * HEAD_DIM
    n_gate = N_BRANCH * GQA
    main_w = 2 * c_lru + attn_w + 6 * kv_w
    segments = ((0, 2 * c_lru, F32, None),
                (2 * c_lru, attn_w, BF16, HEAD_DIM ** -0.5 * LOG2_E),
                (2 * c_lru + attn_w, 2 * kv_w, F32, None),
                (2 * c_lru + attn_w + 2 * kv_w, 4 * kv_w, BF16, None))
    ov_t = _overlap_t(s // CMP_STRIDE, s // SEL_BLOCK)
    row = lambda a: a.reshape(1, -1)

    h = x.reshape(bsz * s, d)
    for l in range(depth):
        w_main = w_in[l].astype(BF16)
        gpad = jnp.zeros((d, LANE - n_gate), F32)
        w_gate = jnp.concatenate([p for hh in range(N_KV_HEADS)
                                  for p in (w_in[l][:, main_w + hh * n_gate:main_w + (hh + 1) * n_gate], gpad)],
                                 axis=1).astype(BF16)
        lru_in, q, kvc, kv4, gates = _in_proj(h, row(g_mix[l]), w_main, segments, w_gate, tm=ROW_TILE)
        lru_n = _lru(lru_in, bsz, s, lru_conv_w[l], row(lru_conv_b[l]), lru_wa[l].astype(BF16),
                        row(lru_ba[l]), lru_wx[l].astype(BF16), row(lru_bx[l]), row(lru_lambda[l]),
                        row(g_lru_out[l]), ts=LRU_TILE)
        pk = (cmp_pe_k[l].reshape(1, -1), cmp_w1_k[l].astype(BF16), row(cmp_b1_k[l]), cmp_w2_k[l].astype(BF16))
        pv = (cmp_pe_v[l].reshape(1, -1), cmp_w1_v[l].astype(BF16), row(cmp_b1_v[l]), cmp_w2_v[l].astype(BF16))
        k_cmp, v_cmp = _compress(kvc, bsz, s, pk, pv)
        attn, w_out_b, w_down_b, w_up_b = _attention_t(q, kv4, gates, k_cmp, v_cmp, ov_t, bsz, s,
                                                     cast=(w_out[l], w_down[l], w_up[l]))
        h1, xn = _out_proj(lru_n, attn, row(g_attn_out[l]), h, w_out_b, row(g_ffn[l]), tm=ROW_TILE)
        h = _ffn(xn, h1, w_up_b, ffn_conv_w[l], row(ffn_conv_b[l]), w_down_b,
                 row(g_final), l == depth - 1, s, tm=ROW_TILE, tf=FF_TILE)
    return h.reshape(bsz, s, d)
```
